```python
import jax, jax.numpy as jnp
from jax import lax
import numpy as np

D_MODEL = 4096
BATCH = 2
SEQ = 8192
DEPTH = 4

CHUNK = 64
N_MIXERS = 3
N_LRU_LAYERS = (DEPTH + 2) // 3
N_SSD_LAYERS = (DEPTH + 1) // 3
N_ATTN_LAYERS = DEPTH // 3
RMS_EPS = 1e-6
FFN_HIDDEN = ((8 * D_MODEL // 3 + 255) // 256) * 256
D_RNN = ((4 * D_MODEL // 3 + 128) // 256) * 256
LRU_BLOCK = 256
LRU_BLOCKS = D_RNN // LRU_BLOCK
LRU_CONV = 4
LRU_C = 8.0
SSD_D_INNER = 2 * D_MODEL
SSD_HEAD_DIM = 64
SSD_HEADS = SSD_D_INNER // SSD_HEAD_DIM
SSD_STATE = 128
SSD_GROUPS = 8
SSD_CONV = 4
SSD_CONV_DIM = SSD_D_INNER + 2 * SSD_GROUPS * SSD_STATE
SSD_IN_DIM = 2 * SSD_D_INNER + 2 * SSD_GROUPS * SSD_STATE + SSD_HEADS
ATTN_HEAD_DIM = 128
ATTN_HEADS = D_MODEL // ATTN_HEAD_DIM
LOOKBACK_CHUNKS = 8
MAX_REL_DIST = 256
MEM_TOKENS = 256
MEM_HEADS = 4
MEM_HEAD_DIM = 128
MEM_WIDTH = MEM_HEADS * MEM_HEAD_DIM

kernel_name = 'hybrid_streaming_rglru_ssd_chunkattn'


def rms_norm(x, g):
    xf = x.astype(jnp.float32)
    y = xf * lax.rsqrt(jnp.mean(xf * xf, axis=-1, keepdims=True) + RMS_EPS)
    return y.astype(x.dtype) * g


def causal_depthwise_conv(x, w, b):
    k = w.shape[0]
    y = lax.conv_general_dilated(x, w[:, None, :], window_strides=(1,), padding=[(k - 1, 0)],
                                 dimension_numbers=('NWC', 'WIO', 'NWC'),
                                 feature_group_count=x.shape[-1])
    return y + b


def block_diag_linear(x, w, b):
    xb = x.reshape(x.shape[:-1] + (w.shape[0], w.shape[1]))
    y = jnp.einsum('btnk,nkj->btnj', xb, w)
    return y.reshape(x.shape) + b


def rglru_mixer(h, w_in, b_in, w_gate, b_gate, conv_w, conv_b, w_a, b_a, w_x, b_x, lam, w_out, b_out):
    f32 = jnp.float32
    gate = jax.nn.gelu(h @ w_gate + b_gate)
    xc = causal_depthwise_conv(h @ w_in + b_in, conv_w, conv_b)
    r = jax.nn.sigmoid(block_diag_linear(xc, w_a, b_a).astype(f32))
    ig = jax.nn.sigmoid(block_diag_linear(xc, w_x, b_x).astype(f32))
    log_a = -LRU_C * r * jax.nn.softplus(-lam.astype(f32))
    a = jnp.exp(log_a)
    u = jnp.sqrt(-jnp.expm1(2.0 * log_a)) * (ig * xc.astype(f32))

    def combine(left, right):
        a_l, u_l = left
        a_r, u_r = right
        return a_l * a_r, a_r * u_l + u_r

    _, hs = lax.associative_scan(combine, (a, u), axis=1)
    return (hs.astype(h.dtype) * gate) @ w_out + b_out


def ssd_chunked(xh, dt, a, bm, cm):
    b_, t_, h_, p_ = xh.shape
    g_, n_ = bm.shape[2], bm.shape[3]
    k_ = h_ // g_
    nc = t_ // CHUNK
    xdt = (xh * dt[..., None]).reshape(b_, nc, CHUNK, g_, k_, p_)
    cum = jnp.cumsum((dt * a).reshape(b_, nc, CHUNK, g_, k_), axis=2)
    bm = bm.reshape(b_, nc, CHUNK, g_, n_)
    cm = cm.reshape(b_, nc, CHUNK, g_, n_)
    causal = jnp.tril(jnp.ones((CHUNK, CHUNK), dtype=bool))
    seg = cum[:, :, :, None] - cum[:, :, None, :]
    decay = jnp.exp(jnp.where(causal[None, None, :, :, None, None], seg, -jnp.inf))
    cb = jnp.einsum('bclgn,bcsgn->bclsg', cm, bm)
    y_diag = jnp.einsum('bclsg,bclsgk,bcsgkp->bclgkp', cb, decay, xdt)
    to_end = jnp.exp(cum[:, :, -1:] - cum)
    states = jnp.einsum('bclgn,bclgk,bclgkp->bcgkpn', bm, to_end, xdt)
    chunk_decay = jnp.exp(cum[:, :, -1])

    def carry_step(state, inp):
        st, dec = inp
        return state * dec[..., None, None] + st, state

    init = jnp.zeros((b_, g_, k_, p_, n_), jnp.float32)
    _, prev = lax.scan(carry_step, init, (jnp.moveaxis(states, 1, 0), jnp.moveaxis(chunk_decay, 1, 0)))
    prev = jnp.moveaxis(prev, 0, 1)
    y_off = jnp.einsum('bclgn,bcgkpn,bclgk->bclgkp', cm, prev, jnp.exp(cum))
    return (y_diag + y_off).reshape(b_, t_, h_, p_)


def mamba2_mixer(h, w_in, conv_w, conv_b, dt_bias, a_log, d_skip, norm_w, w_out):
    f32 = jnp.float32
    b_, t_, _ = h.shape
    proj = h @ w_in
    z, xbc, dt = jnp.split(proj, [SSD_D_INNER, SSD_D_INNER + SSD_CONV_DIM], axis=-1)
    xbc = jax.nn.silu(causal_depthwise_conv(xbc, conv_w, conv_b))
    gn = SSD_GROUPS * SSD_STATE
    xs, bm, cm = jnp.split(xbc, [SSD_D_INNER, SSD_D_INNER + gn], axis=-1)
    xh = xs.reshape(b_, t_, SSD_HEADS, SSD_HEAD_DIM).astype(f32)
    dt = jax.nn.softplus(dt.astype(f32) + dt_bias.astype(f32))
    a = -jnp.exp(a_log.astype(f32))
    y = ssd_chunked(xh, dt, a,
                    bm.reshape(b_, t_, SSD_GROUPS, SSD_STATE).astype(f32),
                    cm.reshape(b_, t_, SSD_GROUPS, SSD_STATE).astype(f32))
    y = y + d_skip.astype(f32)[:, None] * xh
    y = y.reshape(b_, t_, SSD_D_INNER) * jax.nn.silu(z.astype(f32))
    yg = y.reshape(b_, t_, SSD_GROUPS, SSD_D_INNER // SSD_GROUPS)
    yg = yg * lax.rsqrt(jnp.mean(yg * yg, axis=-1, keepdims=True) + RMS_EPS)
    y = yg.reshape(b_, t_, SSD_D_INNER).astype(h.dtype) * norm_w
    return y @ w_out


def chunk_band_attention(h, wq, wk, wv, wo, rel_table):
    f32 = jnp.float32
    b_, t_, d_ = h.shape
    nc = t_ // CHUNK
    pad = LOOKBACK_CHUNKS * CHUNK
    band = (LOOKBACK_CHUNKS + 1) * CHUNK
    q = (h @ wq).reshape(b_, nc, CHUNK, ATTN_HEADS, ATTN_HEAD_DIM) * ATTN_HEAD_DIM ** -0.5
    k = (h @ wk).reshape(b_, t_, ATTN_HEADS, ATTN_HEAD_DIM)
    v = (h @ wv).reshape(b_, t_, ATTN_HEADS, ATTN_HEAD_DIM)
    kp = jnp.pad(k, ((0, 0), (pad, 0), (0, 0), (0, 0)))
    vp = jnp.pad(v, ((0, 0), (pad, 0), (0, 0), (0, 0)))
    rel = (jnp.arange(CHUNK)[:, None] + pad) - jnp.arange(band)[None, :]
    idx = jnp.clip(rel, -MAX_REL_DIST, MAX_REL_DIST) + MAX_REL_DIST
    bias = rel_table.astype(f32)[:, idx]

    def one_chunk(inp):
        qc, start = inp
        kb = lax.dynamic_slice_in_dim(kp, start, band, axis=1)
        vb = lax.dynamic_slice_in_dim(vp, start, band, axis=1)
        s = jnp.einsum('bqhd,bkhd->bhqk', qc, kb).astype(f32) + bias
        valid = (start - pad + jnp.arange(band)) >= 0
        s = jnp.where(valid[None, None, None, :], s, -jnp.inf)
        p = jax.nn.softmax(s, axis=-1).astype(vb.dtype)
        return jnp.einsum('bhqk,bkhd->bqhd', p, vb)

    starts = jnp.arange(nc, dtype=jnp.int32) * CHUNK
    o = lax.map(one_chunk, (jnp.moveaxis(q, 1, 0), starts))
    o = jnp.moveaxis(o, 0, 1).reshape(b_, t_, d_)
    return o @ wo


def memory_cross_attention(h, m, wq, wk, wv, wo):
    b_, t_, _ = h.shape
    q = (h @ wq).reshape(b_, t_, MEM_HEADS, MEM_HEAD_DIM)
    k = (m @ wk).reshape(b_, m.shape[1], MEM_HEADS, MEM_HEAD_DIM)
    v = (m @ wv).reshape(b_, m.shape[1], MEM_HEADS, MEM_HEAD_DIM)
    s = jnp.einsum('bqhd,bkhd->bhqk', q, k).astype(jnp.float32) * MEM_HEAD_DIM ** -0.5
    p = jax.nn.softmax(s, axis=-1).astype(v.dtype)
    o = jnp.einsum('bhqk,bkhd->bqhd', p, v).reshape(b_, t_, MEM_WIDTH)
    return o @ wo


def swiglu(h, w_gate, w_up, w_down):
    return (jax.nn.silu(h @ w_gate) * (h @ w_up)) @ w_down


def setup_inputs(seed: int = 0) -> dict:
    key = jax.random.key(seed)
    keys = iter(jax.random.split(key, 64))
    f32 = jnp.float32

    def nrm(shape, fan_in):
        return jax.random.normal(next(keys), shape, f32) * (fan_in ** -0.5)

    def gain(shape):
        return 1.0 + 0.05 * jax.random.normal(next(keys), shape, f32)

    def small(shape, scale=0.01):
        return scale * jax.random.normal(next(keys), shape, f32)

    D, L = D_MODEL, DEPTH
    NA, NS, NT = N_LRU_LAYERS, N_SSD_LAYERS, N_ATTN_LAYERS
    R = D_RNN
    inp = {}
    inp['x'] = jax.random.normal(next(keys), (BATCH, SEQ, D), f32)
    inp['mem'] = jax.random.normal(next(keys), (BATCH, MEM_TOKENS, D), f32)
    inp['norm_mix'] = gain((L, D))
    inp['norm_mem_q'] = gain((L, D))
    inp['norm_mem_kv'] = gain((L, D))
    inp['norm_ffn'] = gain((L, D))
    inp['norm_final'] = gain((D,))
    inp['mem_wq'] = nrm((L, D, MEM_WIDTH), D)
    inp['mem_wk'] = nrm((L, D, MEM_WIDTH), D)
    inp['mem_wv'] = nrm((L, D, MEM_WIDTH), D)
    inp['mem_wo'] = nrm((L, MEM_WIDTH, D), MEM_WIDTH)
    inp['ffn_w_gate'] = nrm((L, D, FFN_HIDDEN), D)
    inp['ffn_w_up'] = nrm((L, D, FFN_HIDDEN), D)
    inp['ffn_w_down'] = nrm((L, FFN_HIDDEN, D), FFN_HIDDEN)
    inp['lru_w_in'] = nrm((NA, D, R), D)
    inp['lru_b_in'] = small((NA, R))
    inp['lru_w_gate'] = nrm((NA, D, R), D)
    inp['lru_b_gate'] = small((NA, R))
    inp['lru_conv_w'] = nrm((NA, LRU_CONV, R), LRU_CONV)
    inp['lru_conv_b'] = small((NA, R))
    inp['lru_w_a'] = nrm((NA, LRU_BLOCKS, LRU_BLOCK, LRU_BLOCK), LRU_BLOCK)
    inp['lru_b_a'] = small((NA, R))
    inp['lru_w_x'] = nrm((NA, LRU_BLOCKS, LRU_BLOCK, LRU_BLOCK), LRU_BLOCK)
    inp['lru_b_x'] = small((NA, R))
    a_c = jax.random.uniform(next(keys), (NA, R), f32, minval=0.9, maxval=0.999)
    a0 = a_c ** (1.0 / LRU_C)
    inp['lru_lambda'] = jnp.log(a0) - jnp.log1p(-a0)
    inp['lru_w_out'] = nrm((NA, R, D), R)
    inp['lru_b_out'] = small((NA, D))
    inp['ssd_w_in'] = nrm((NS, D, SSD_IN_DIM), D)
    inp['ssd_conv_w'] = nrm((NS, SSD_CONV, SSD_CONV_DIM), SSD_CONV)
    inp['ssd_conv_b'] = small((NS, SSD_CONV_DIM))
    dt0 = jnp.exp(jax.random.uniform(next(keys), (NS, SSD_HEADS), f32,
                                     minval=float(np.log(1e-3)), maxval=float(np.log(1e-1))))
    inp['ssd_dt_bias'] = dt0 + jnp.log(-jnp.expm1(-dt0))
    inp['ssd_a_log'] = jnp.log(jax.random.uniform(next(keys), (NS, SSD_HEADS), f32, minval=1.0, maxval=16.0))
    inp['ssd_d'] = gain((NS, SSD_HEADS))
    inp['ssd_norm'] = gain((NS, SSD_D_INNER))
    inp['ssd_w_out'] = nrm((NS, SSD_D_INNER, D), SSD_D_INNER)
    inp['attn_wq'] = nrm((NT, D, D), D)
    inp['attn_wk'] = nrm((NT, D, D), D)
    inp['attn_wv'] = nrm((NT, D, D), D)
    inp['attn_wo'] = nrm((NT, D, D), D)
    inp['attn_rel_bias'] = small((NT, ATTN_HEADS, 2 * MAX_REL_DIST + 1), 0.1)
    return inp


def reference(x, mem, norm_mix, norm_mem_q, norm_mem_kv, norm_ffn, norm_final,
              mem_wq, mem_wk, mem_wv, mem_wo, ffn_w_gate, ffn_w_up, ffn_w_down,
              lru_w_in, lru_b_in, lru_w_gate, lru_b_gate, lru_conv_w, lru_conv_b,
              lru_w_a, lru_b_a, lru_w_x, lru_b_x, lru_lambda, lru_w_out, lru_b_out,
              ssd_w_in, ssd_conv_w, ssd_conv_b, ssd_dt_bias, ssd_a_log, ssd_d, ssd_norm, ssd_w_out,
              attn_wq, attn_wk, attn_wv, attn_wo, attn_rel_bias):
    for i in range(DEPTH):
        kind, j = i % N_MIXERS, i // N_MIXERS
        h = rms_norm(x, norm_mix[i])
        if kind == 0:
            y = rglru_mixer(h, lru_w_in[j], lru_b_in[j], lru_w_gate[j], lru_b_gate[j],
                            lru_conv_w[j], lru_conv_b[j], lru_w_a[j], lru_b_a[j],
                            lru_w_x[j], lru_b_x[j], lru_lambda[j], lru_w_out[j], lru_b_out[j])
        elif kind == 1:
            y = mamba2_mixer(h, ssd_w_in[j], ssd_conv_w[j], ssd_conv_b[j], ssd_dt_bias[j],
                             ssd_a_log[j], ssd_d[j], ssd_norm[j], ssd_w_out[j])
        else:
            y = chunk_band_attention(h, attn_wq[j], attn_wk[j], attn_wv[j], attn_wo[j], attn_rel_bias[j])
        x = x + y
        x = x + memory_cross_attention(rms_norm(x, norm_mem_q[i]), rms_norm(mem, norm_mem_kv[i]),
                                       mem_wq[i], mem_wk[i], mem_wv[i], mem_wo[i])
        x = x + swiglu(rms_norm(x, norm_ffn[i]), ffn_w_gate[i], ffn_w_up[i], ffn_w_down[i])
    return rms_norm(x, norm_final)
```

```python
import functools

import numpy as np
import jax
import jax.numpy as jnp
from jax import lax
from jax.experimental import pallas as pl
from jax.experimental.pallas import tpu as pltpu

F32 = jnp.float32
BF16 = jnp.bfloat16

V7X_LANES = 128
V7X_SUBLANES = 8
V7X_VMEM_BYTES = 64 * 2**20
V7X_VMEM_LIMIT_CAP = V7X_VMEM_BYTES - 6 * 2**20
COMPILER_SCRATCH_ALLOWANCE = 4 * 2**20

RMS_EPS = 1e-6
N_MIXERS = 3
CHUNK = 64
LRU_C = 8.0
LRU_CONV = 4
SSD_HEAD_DIM = 64
SSD_STATE = 128
SSD_GROUPS = 8
SSD_CONV = 4
ATTN_HEAD_DIM = 128
LOOKBACK_CHUNKS = 8
MAX_REL_DIST = 256
MEM_HEADS = 4
MEM_HEAD_DIM = 128
CONV_PAD_ROWS = V7X_SUBLANES


def _tile(dim, target, align):
    if dim <= target:
        return dim
    best = None
    for t in range(align, target + 1, align):
        if dim % t == 0:
            best = t
    assert best is not None, (dim, target, align)
    return best


def _params(semantics, vmem_bytes):
    limit = vmem_bytes + COMPILER_SCRATCH_ALLOWANCE
    return pltpu.CompilerParams(
        dimension_semantics=semantics,
        vmem_limit_bytes=int(min(max(limit, 16 * 2**20), V7X_VMEM_LIMIT_CAP)),
    )


def _nbytes(shape, dtype):
    return int(np.prod(shape)) * jnp.dtype(dtype).itemsize


def _rms(x, g):
    ms = jnp.mean(x * x, axis=-1, keepdims=True)
    return x * lax.rsqrt(ms + RMS_EPS) * g


def _softmax_rows(s):
    m = jnp.max(s, axis=-1, keepdims=True)
    e = jnp.exp(s - m)
    return e / jnp.sum(e, axis=-1, keepdims=True)


def _rmsnorm_kernel(x_ref, g_ref, o_ref):
    o_ref[...] = _rms(x_ref[...], g_ref[...]).astype(o_ref.dtype)


def _rmsnorm(x, g, out_dtype):
    n, d = x.shape
    bm = _tile(n, 256, V7X_SUBLANES)
    vmem = 2 * (_nbytes((bm, d), F32) + _nbytes((bm, d), out_dtype)) + 2 * _nbytes((bm, d), F32)
    return pl.pallas_call(
        _rmsnorm_kernel,
        out_shape=jax.ShapeDtypeStruct((n, d), out_dtype),
        grid=(n // bm,),
        in_specs=[pl.BlockSpec((bm, d), lambda i: (i, 0)),
                  pl.BlockSpec((1, d), lambda i: (0, 0))],
        out_specs=pl.BlockSpec((bm, d), lambda i: (i, 0)),
        compiler_params=_params(("parallel",), vmem),
        name="rmsnorm",
    )(x, g.reshape(1, d))


def _mm_kernel(x_ref, w_ref, *rest, nk, has_bias, has_res, act, scale):
    rest = list(rest)
    b_ref = rest.pop(0) if has_bias else None
    r_ref = rest.pop(0) if has_res else None
    o_ref = rest.pop(0)
    acc_ref = rest.pop(0) if nk > 1 else None

    def epilogue(acc):
        if has_bias:
            acc = acc + b_ref[...]
        if scale is not None:
            acc = acc * scale
        if act == "gelu":
            acc = jax.nn.gelu(acc)
        if has_res:
            acc = acc + r_ref[...]
        o_ref[...] = acc.astype(o_ref.dtype)

    part = jnp.dot(x_ref[...], w_ref[...], preferred_element_type=F32)
    if nk == 1:
        epilogue(part)
    else:
        k = pl.program_id(2)

        @pl.when(k == 0)
        def _():
            acc_ref[...] = part

        @pl.when(k > 0)
        def _():
            acc_ref[...] += part

        @pl.when(k == nk - 1)
        def _():
            epilogue(acc_ref[...])


def _matmul(x, w, *, bias=None, residual=None, act=None, scale=None, out_dtype=F32,
            bm=1024, bn=1024, bk=4096, name="matmul"):
    m, kdim = x.shape
    n = w.shape[1]
    bm = _tile(m, bm, V7X_SUBLANES * 2)
    bn = _tile(n, bn, V7X_LANES)
    bk = _tile(kdim, bk, V7X_LANES)
    nk = kdim // bk
    in_specs = [pl.BlockSpec((bm, bk), lambda i, j, k: (i, k)),
                pl.BlockSpec((bk, bn), lambda i, j, k: (k, j))]
    args = [x, w]
    vmem = 2 * (_nbytes((bm, bk), x.dtype) + _nbytes((bk, bn), w.dtype) + _nbytes((bm, bn), out_dtype))
    vmem += (2 + 2 * (act is not None)) * _nbytes((bm, bn), F32)
    if bias is not None:
        in_specs.append(pl.BlockSpec((1, bn), lambda i, j, k: (0, j)))
        args.append(bias.reshape(1, n).astype(F32))
    if residual is not None:
        in_specs.append(pl.BlockSpec((bm, bn), lambda i, j, k: (i, j)))
        args.append(residual)
        vmem += 2 * _nbytes((bm, bn), residual.dtype)
    scratch = []
    if nk > 1:
        scratch.append(pltpu.VMEM((bm, bn), F32))
        vmem += _nbytes((bm, bn), F32)
    return pl.pallas_call(
        functools.partial(_mm_kernel, nk=nk, has_bias=bias is not None,
                          has_res=residual is not None, act=act, scale=scale),
        out_shape=jax.ShapeDtypeStruct((m, n), out_dtype),
        grid=(m // bm, n // bn, nk),
        in_specs=in_specs,
        out_specs=pl.BlockSpec((bm, bn), lambda i, j, k: (i, j)),
        scratch_shapes=scratch,
        compiler_params=_params(("parallel", "parallel", "arbitrary"), vmem),
        name=name,
    )(*args)


def _swiglu_up_kernel(x_ref, wg_ref, wu_ref, o_ref):
    x = x_ref[...]
    g = jnp.dot(x, wg_ref[...], preferred_element_type=F32)
    u = jnp.dot(x, wu_ref[...], preferred_element_type=F32)
    o_ref[...] = (jax.nn.silu(g) * u).astype(o_ref.dtype)


def _swiglu_up(x, wg, wu, *, bm=1024, bn=512):
    m, kdim = x.shape
    n = wg.shape[1]
    bm = _tile(m, bm, V7X_SUBLANES * 2)
    bn = _tile(n, bn, V7X_LANES)
    vmem = 2 * (_nbytes((bm, kdim), BF16) + 2 * _nbytes((kdim, bn), BF16) + _nbytes((bm, bn), BF16))
    vmem += 5 * _nbytes((bm, bn), F32)
    return pl.pallas_call(
        _swiglu_up_kernel,
        out_shape=jax.ShapeDtypeStruct((m, n), BF16),
        grid=(m // bm, n // bn),
        in_specs=[pl.BlockSpec((bm, kdim), lambda i, j: (i, 0)),
                  pl.BlockSpec((kdim, bn), lambda i, j: (0, j)),
                  pl.BlockSpec((kdim, bn), lambda i, j: (0, j))],
        out_specs=pl.BlockSpec((bm, bn), lambda i, j: (i, j)),
        compiler_params=_params(("parallel", "parallel"), vmem),
        name="swiglu_up",
    )(x, wg, wu)


def _memattn_kernel(x_ref, gq_ref, wq_ref, k_ref, v_ref, wo_ref, gf_ref, xo_ref, ho_ref):
    x = x_ref[...]
    h = _rms(x, gq_ref[...]).astype(BF16)
    q = jnp.dot(h, wq_ref[...], preferred_element_type=F32).astype(BF16)
    k = k_ref[0]
    v = v_ref[0]
    scale = MEM_HEAD_DIM ** -0.5
    outs = []
    for hh in range(MEM_HEADS):
        sl = slice(hh * MEM_HEAD_DIM, (hh + 1) * MEM_HEAD_DIM)
        s = lax.dot_general(q[:, sl], k[:, sl], (((1,), (1,)), ((), ())),
                            preferred_element_type=F32) * scale
        p = _softmax_rows(s).astype(BF16)
        outs.append(jnp.dot(p, v[:, sl], preferred_element_type=F32).astype(BF16))
    o = jnp.concatenate(outs, axis=1)
    x2 = x + jnp.dot(o, wo_ref[...], preferred_element_type=F32)
    xo_ref[...] = x2
    ho_ref[...] = _rms(x2, gf_ref[...]).astype(BF16)


def _mem_xattn(x, rows_per_batch, kmem, vmem_kv, g_q, wq, wo, g_ffn):
    n, d = x.shape
    nb, tm, width = kmem.shape
    bm = _tile(rows_per_batch, 256, V7X_SUBLANES * 2)
    per_batch = rows_per_batch // bm
    vmem = 2 * (2 * _nbytes((bm, d), F32) + _nbytes((bm, d), BF16) + 2 * _nbytes((d, width), BF16)
                + 2 * _nbytes((tm, width), BF16))
    vmem += 4 * _nbytes((bm, d), F32)
    return pl.pallas_call(
        _memattn_kernel,
        out_shape=(jax.ShapeDtypeStruct((n, d), F32), jax.ShapeDtypeStruct((n, d), BF16)),
        grid=(n // bm,),
        in_specs=[pl.BlockSpec((bm, d), lambda i: (i, 0)),
                  pl.BlockSpec((1, d), lambda i: (0, 0)),
                  pl.BlockSpec((d, width), lambda i: (0, 0)),
                  pl.BlockSpec((1, tm, width), lambda i: (i // per_batch, 0, 0)),
                  pl.BlockSpec((1, tm, width), lambda i: (i // per_batch, 0, 0)),
                  pl.BlockSpec((width, d), lambda i: (0, 0)),
                  pl.BlockSpec((1, d), lambda i: (0, 0))],
        out_specs=(pl.BlockSpec((bm, d), lambda i: (i, 0)),
                   pl.BlockSpec((bm, d), lambda i: (i, 0))),
        compiler_params=_params(("parallel",), vmem),
        name="mem_xattn",
    )(x, g_q.reshape(1, d), wq, kmem, vmem_kv, wo, g_ffn.reshape(1, d))


def _lru_kernel(xin_ref, gate_ref, cw_ref, cb_ref, wax_ref, bax_ref, lam_ref, o_ref,
                xs_ref, a_ref, u_ref, h_ref, *, tb, nblk, blk):
    t = pl.program_id(2)
    pad = CONV_PAD_ROWS

    @pl.when(t == 0)
    def _():
        xs_ref[0:pad, :] = jnp.zeros((pad, xs_ref.shape[1]), F32)
        h_ref[...] = jnp.zeros(h_ref.shape, F32)

    @pl.when(t > 0)
    def _():
        xs_ref[0:pad, :] = xs_ref[tb:tb + pad, :]

    xs_ref[pad:pad + tb, :] = xin_ref[...]
    cw = cw_ref[...]
    xc = cb_ref[...]
    for j in range(LRU_CONV):
        off = pad - (LRU_CONV - 1) + j
        xc = xc + cw[j:j + 1, :] * xs_ref[off:off + tb, :]

    neg_sp = -LRU_C * jax.nn.softplus(-lam_ref[...])
    for nb in range(nblk):
        sl = slice(nb * blk, (nb + 1) * blk)
        xcb = xc[:, sl]
        gates = jnp.dot(xcb.astype(BF16), wax_ref[nb], preferred_element_type=F32) + bax_ref[nb]
        r = jax.nn.sigmoid(gates[:, :blk])
        ig = jax.nn.sigmoid(gates[:, blk:])
        log_a = neg_sp[:, sl] * r
        th = jnp.tanh(log_a)
        one_minus_a2 = -2.0 * th / (1.0 - th)
        a_ref[:, sl] = jnp.exp(log_a)
        u_ref[:, sl] = jnp.sqrt(one_minus_a2) * (ig * xcb)

    width = a_ref.shape[1]
    row = lax.broadcasted_iota(jnp.int32, (V7X_SUBLANES, width), 0)

    def scan_rows(i, h):
        r0 = pl.multiple_of(i * V7X_SUBLANES, V7X_SUBLANES)
        a = a_ref[pl.ds(r0, V7X_SUBLANES), :]
        u = u_ref[pl.ds(r0, V7X_SUBLANES), :]
        for sh in (1, 2, 4):
            a_prev = jnp.where(row >= sh, pltpu.roll(a, sh, axis=0), 1.0)
            u_prev = jnp.where(row >= sh, pltpu.roll(u, sh, axis=0), 0.0)
            u = a * u_prev + u
            a = a * a_prev
        hs = u + a * h
        u_ref[pl.ds(r0, V7X_SUBLANES), :] = hs
        return hs[V7X_SUBLANES - 1:V7X_SUBLANES, :]

    h_ref[...] = lax.fori_loop(0, tb // V7X_SUBLANES, scan_rows, h_ref[...])
    o_ref[...] = (u_ref[...] * gate_ref[...].astype(F32)).astype(o_ref.dtype)


def _lru_core(xin, gate, rows_per_batch, conv_w, conv_b, w_ax, b_ax, lam):
    n, r = xin.shape
    nblocks, blk = w_ax.shape[0], w_ax.shape[1]
    nblk = _tile(nblocks, 3, 1)
    c = nblk * blk
    tb = _tile(rows_per_batch, 512, V7X_SUBLANES * 2)
    nt = rows_per_batch // tb
    nbat = n // rows_per_batch
    pad = CONV_PAD_ROWS
    vmem = 2 * (_nbytes((tb, c), F32) + 2 * _nbytes((tb, c), BF16) + _nbytes((nblk, blk, 2 * blk), BF16))
    vmem += _nbytes((tb + pad, c), F32) + 2 * _nbytes((tb, c), F32) + 6 * _nbytes((tb, c), F32)
    return pl.pallas_call(
        functools.partial(_lru_kernel, tb=tb, nblk=nblk, blk=blk),
        out_shape=jax.ShapeDtypeStruct((n, r), BF16),
        grid=(nbat, r // c, nt),
        in_specs=[pl.BlockSpec((tb, c), lambda b, j, t: (b * nt + t, j)),
                  pl.BlockSpec((tb, c), lambda b, j, t: (b * nt + t, j)),
                  pl.BlockSpec((LRU_CONV, c), lambda b, j, t: (0, j)),
                  pl.BlockSpec((1, c), lambda b, j, t: (0, j)),
                  pl.BlockSpec((nblk, blk, 2 * blk), lambda b, j, t: (j, 0, 0)),
                  pl.BlockSpec((nblk, 1, 2 * blk), lambda b, j, t: (j, 0, 0)),
                  pl.BlockSpec((1, c), lambda b, j, t: (0, j))],
        out_specs=pl.BlockSpec((tb, c), lambda b, j, t: (b * nt + t, j)),
        scratch_shapes=[pltpu.VMEM((tb + pad, c), F32),
                        pltpu.VMEM((tb, c), F32),
                        pltpu.VMEM((tb, c), F32),
                        pltpu.VMEM((1, c), F32)],
        compiler_params=_params(("parallel", "parallel", "arbitrary"), vmem),
        name="lru_core",
    )(xin, gate, conv_w, conv_b.reshape(1, r), w_ax, b_ax, lam.reshape(1, r))


def _lru_layer(x, h, rows_per_batch, w_in, b_in, w_gate, b_gate, conv_w, conv_b,
               w_a, b_a, w_x, b_x, lam, w_out, b_out):
    nblocks, blk = w_a.shape[0], w_a.shape[1]
    gate = _matmul(h, w_gate.astype(BF16), bias=b_gate, act="gelu", out_dtype=BF16,
                   bn=896, name="lru_gate")
    xin = _matmul(h, w_in.astype(BF16), bias=b_in, out_dtype=F32, bn=896, name="lru_in")
    w_ax = jnp.concatenate([w_a, w_x], axis=-1).astype(BF16)
    b_ax = jnp.concatenate([b_a.reshape(nblocks, 1, blk), b_x.reshape(nblocks, 1, blk)], axis=-1)
    hs = _lru_core(xin, gate, rows_per_batch, conv_w, conv_b, w_ax, b_ax, lam)
    return _matmul(hs, w_out.astype(BF16), bias=b_out, residual=x, out_dtype=F32,
                   bk=2688, name="lru_out")


def _split3(v):
    hi = v.astype(BF16)
    r1 = v - hi.astype(F32)
    mid = r1.astype(BF16)
    lo = (r1 - mid.astype(F32)).astype(BF16)
    return hi, mid, lo


def _ssd_kernel(z_ref, xbc_ref, dtr_ref, cw_ref, cb_ref, dtb_ref, alog_ref, dsk_ref, nw_ref, y_ref,
                xs_ref, xa_ref, st_ref, ct2_ref, dt2_ref, wt2_ref, *, d_inner):
    c = pl.program_id(1)
    L = CHUNK
    pad = CONV_PAD_ROWS
    lanes = V7X_LANES
    half = lanes // 2
    gw = d_inner // SSD_GROUPS
    pairs_per_group = gw // lanes
    npair = d_inner // lanes
    assert SSD_HEAD_DIM == half and SSD_STATE == lanes and 2 * npair == lanes and L == half

    @pl.when(c == 0)
    def _():
        xs_ref[0:pad, :] = jnp.zeros((pad, xs_ref.shape[1]), F32)
        st_ref[...] = jnp.zeros(st_ref.shape, F32)

    @pl.when(c > 0)
    def _():
        xs_ref[0:pad, :] = xs_ref[L:L + pad, :]

    xs_ref[pad:pad + L, :] = xbc_ref[...]
    cw = cw_ref[...]
    xc = cb_ref[...]
    for j in range(SSD_CONV):
        off = pad - (SSD_CONV - 1) + j
        xc = xc + cw[j:j + 1, :] * xs_ref[off:off + L, :]
    xa_ref[...] = jax.nn.silu(xc)

    dt = jax.nn.softplus(dtr_ref[...] + dtb_ref[...])
    da = dt * (-jnp.exp(alog_ref[...]))
    ri = lax.broadcasted_iota(jnp.int32, (L, L), 0)
    ci = lax.broadcasted_iota(jnp.int32, (L, L), 1)
    tril = jnp.where(ri >= ci, 1.0, 0.0).astype(BF16)
    parts = jnp.dot(tril, jnp.concatenate(_split3(da), axis=1), preferred_element_type=F32)
    cum = (parts[:, :lanes] + parts[:, lanes:2 * lanes]) + parts[:, 2 * lanes:]
    last = cum[L - 1:L, :]
    w_end = dt * jnp.exp(last - cum)
    dec_chunk = jnp.exp(last)

    def pair_rows(v):
        vt = v.T
        return jnp.concatenate([vt[:npair], vt[npair:]], axis=1)

    ct2_ref[...] = pair_rows(cum)
    dt2_ref[...] = pair_rows(dt)
    wt2_ref[...] = pair_rows(w_end)

    lane_i = lax.broadcasted_iota(jnp.int32, (L, lanes), 1)
    row_i = lax.broadcasted_iota(jnp.int32, (L, lanes), 0)
    lo_half = lane_i < half
    causal2 = row_i >= jnp.where(lo_half, lane_i, lane_i - half)
    lo_half_row = lax.broadcasted_iota(jnp.int32, (1, lanes), 1) < half
    lo_half_sq = lax.broadcasted_iota(jnp.int32, (lanes, lanes), 1) < half
    top_half_sq = lax.broadcasted_iota(jnp.int32, (lanes, lanes), 0) < half
    blockdiag = lo_half_sq == top_half_sq

    def group_body(g, carry):
        x0 = pl.multiple_of(g * gw, gw)
        b0 = pl.multiple_of(d_inner + g * SSD_STATE, SSD_STATE)
        c0 = pl.multiple_of(d_inner + SSD_GROUPS * SSD_STATE + g * SSD_STATE, SSD_STATE)
        p0 = pl.multiple_of(g * pairs_per_group, pairs_per_group)
        bg = xa_ref[:, pl.ds(b0, SSD_STATE)]
        cg = xa_ref[:, pl.ds(c0, SSD_STATE)].astype(BF16)
        b2 = jnp.concatenate([bg, bg], axis=0)
        cb2 = lax.dot_general(cg, b2.astype(BF16), (((1,), (1,)), ((), ())),
                              preferred_element_type=F32)
        bt2 = b2.T
        sg = st_ref[g]
        yoff = jnp.dot(cg, sg.astype(BF16), preferred_element_type=F32)
        shift = lax.rem(lanes - g * pairs_per_group, lanes)
        cum_r = pltpu.roll(cum, shift, axis=1)
        dec_r = pltpu.roll(dec_chunk, shift, axis=1)
        ct8 = ct2_ref[pl.ds(p0, pairs_per_group), :]
        dt8 = dt2_ref[pl.ds(p0, pairs_per_group), :]
        wt8 = wt2_ref[pl.ds(p0, pairs_per_group), :]
        xg = xa_ref[:, pl.ds(x0, gw)]
        ys, snew = [], []
        for pj in range(pairs_per_group):
            ls = slice(pj * lanes, (pj + 1) * lanes)
            xp = xg[:, ls]
            colp = jnp.where(lo_half,
                             jnp.broadcast_to(cum_r[:, pj:pj + 1], (L, lanes)),
                             jnp.broadcast_to(cum_r[:, half + pj:half + pj + 1], (L, lanes)))
            seg = jnp.where(causal2, colp - ct8[pj:pj + 1, :], -jnp.inf)
            m_diag = (cb2 * jnp.exp(seg) * dt8[pj:pj + 1, :]).astype(BF16)
            m_state = (bt2 * wt8[pj:pj + 1, :]).astype(BF16)
            x2 = jnp.concatenate([xp, xp], axis=0)
            xbd = jnp.where(blockdiag, x2, 0.0).astype(BF16)
            res = jnp.dot(jnp.concatenate([m_diag, m_state], axis=0), xbd,
                          preferred_element_type=F32)
            ys.append(res[:L] + jnp.exp(colp) * yoff[:, ls])
            decp = jnp.where(lo_half_row,
                             jnp.broadcast_to(dec_r[:, pj:pj + 1], (1, lanes)),
                             jnp.broadcast_to(dec_r[:, half + pj:half + pj + 1], (1, lanes)))
            snew.append(decp * sg[:, ls] + res[L:])
        st_ref[g] = jnp.concatenate(snew, axis=1)
        y = jnp.concatenate(ys, axis=1) + dsk_ref[:, pl.ds(x0, gw)] * xg
        y = y * jax.nn.silu(z_ref[:, pl.ds(x0, gw)].astype(F32))
        y = _rms(y, nw_ref[:, pl.ds(x0, gw)])
        y_ref[:, pl.ds(x0, gw)] = y.astype(y_ref.dtype)
        return carry

    lax.fori_loop(0, SSD_GROUPS, group_body, 0)


def _ssd_core(z, xbc, dtr, rows_per_batch, conv_w, conv_b, dt_bias, a_log, d_rep, norm_w):
    n, d_inner = z.shape
    cdim = xbc.shape[1]
    nheads = dtr.shape[1]
    nc = rows_per_batch // CHUNK
    nbat = n // rows_per_batch
    gw = d_inner // SSD_GROUPS
    npair = d_inner // V7X_LANES
    pad = CONV_PAD_ROWS
    vmem = 2 * (2 * _nbytes((CHUNK, d_inner), BF16) + _nbytes((CHUNK, cdim), F32)
                + _nbytes((SSD_CONV + 1, cdim), F32))
    vmem += _nbytes((CHUNK + pad, cdim), F32) + _nbytes((CHUNK, cdim), F32)
    vmem += _nbytes((SSD_GROUPS, SSD_STATE, gw), F32) + 8 * _nbytes((SSD_STATE, gw), F32)
    row = lambda b, c: (b * nc + c, 0)
    const = lambda b, c: (0, 0)
    return pl.pallas_call(
        functools.partial(_ssd_kernel, d_inner=d_inner),
        out_shape=jax.ShapeDtypeStruct((n, d_inner), BF16),
        grid=(nbat, nc),
        in_specs=[pl.BlockSpec((CHUNK, d_inner), row),
                  pl.BlockSpec((CHUNK, cdim), row),
                  pl.BlockSpec((CHUNK, nheads), row),
                  pl.BlockSpec((SSD_CONV, cdim), const),
                  pl.BlockSpec((1, cdim), const),
                  pl.BlockSpec((1, nheads), const),
                  pl.BlockSpec((1, nheads), const),
                  pl.BlockSpec((1, d_inner), const),
                  pl.BlockSpec((1, d_inner), const)],
        out_specs=pl.BlockSpec((CHUNK, d_inner), row),
        scratch_shapes=[pltpu.VMEM((CHUNK + pad, cdim), F32),
                        pltpu.VMEM((CHUNK, cdim), F32),
                        pltpu.VMEM((SSD_GROUPS, SSD_STATE, gw), F32),
                        pltpu.VMEM((npair, V7X_LANES), F32),
                        pltpu.VMEM((npair, V7X_LANES), F32),
                        pltpu.VMEM((npair, V7X_LANES), F32)],
        compiler_params=_params(("parallel", "arbitrary"), vmem),
        name="ssd_core",
    )(z, xbc, dtr, conv_w, conv_b.reshape(1, cdim), dt_bias.reshape(1, nheads),
      a_log.reshape(1, nheads), d_rep.reshape(1, d_inner), norm_w.reshape(1, d_inner))


def _ssd_layer(x, h, rows_per_batch, w_in, conv_w, conv_b, dt_bias, a_log, d_skip, norm_w, w_out):
    nheads = dt_bias.shape[0]
    d_inner = nheads * SSD_HEAD_DIM
    cdim = conv_w.shape[1]
    perm = np.concatenate([np.arange(0, nheads, 2), np.arange(1, nheads, 2)])
    w_z = w_in[:, :d_inner].astype(BF16)
    w_xbc = w_in[:, d_inner:d_inner + cdim].astype(BF16)
    w_dt = w_in[:, d_inner + cdim:][:, perm].astype(BF16)
    z = _matmul(h, w_z, out_dtype=BF16, name="ssd_in_z")
    xbc = _matmul(h, w_xbc, out_dtype=F32, name="ssd_in_xbc")
    dtr = _matmul(h, w_dt, out_dtype=F32, name="ssd_in_dt")
    d_rep = jnp.repeat(d_skip.astype(F32), SSD_HEAD_DIM)
    y = _ssd_core(z, xbc, dtr, rows_per_batch, conv_w, conv_b, dt_bias[perm], a_log[perm], d_rep, norm_w)
    return _matmul(y, w_out.astype(BF16), residual=x, out_dtype=F32, bk=2048, name="ssd_out")


def _bias_gather_kernel(idx_ref, tab_ref, o_ref, *, heads):
    idx = idx_ref[...]
    ntab = tab_ref.shape[1]
    rows = lax.broadcasted_iota(jnp.int32, (ntab, idx.shape[1]), 0)
    onehot = jnp.where(rows == idx, 1.0, 0.0).astype(BF16)
    r = jnp.dot(tab_ref[...], onehot, preferred_element_type=F32)
    o_ref[...] = (r[0:heads] + r[heads:2 * heads]) + r[2 * heads:3 * heads]


def _rel_bias(rel_table):
    heads, ntab = rel_table.shape
    pad = LOOKBACK_CHUNKS * CHUNK
    band = pad + CHUNK
    rel = (np.arange(CHUNK)[:, None] + pad) - np.arange(band)[None, :]
    idx = (np.clip(rel, -MAX_REL_DIST, MAX_REL_DIST) + MAX_REL_DIST).astype(np.int32).reshape(1, -1)
    npos = idx.shape[1]
    ntab_p = -(-ntab // V7X_LANES) * V7X_LANES
    tab = jnp.pad(rel_table.astype(F32), ((0, 0), (0, ntab_p - ntab)))
    tab3 = jnp.concatenate(_split3(tab), axis=0)
    tp = _tile(npos, 2048, V7X_LANES)
    out = pl.pallas_call(
        functools.partial(_bias_gather_kernel, heads=heads),
        out_shape=jax.ShapeDtypeStruct((heads, npos), F32),
        grid=(npos // tp,),
        in_specs=[pl.BlockSpec((1, tp), lambda i: (0, i)),
                  pl.BlockSpec((3 * heads, ntab_p), lambda i: (0, 0))],
        out_specs=pl.BlockSpec((heads, tp), lambda i: (0, i)),
        compiler_params=_params(("parallel",), 16 * 2**20),
        name="rel_bias_gather",
    )(jnp.asarray(idx), tab3)
    return out.reshape(heads, CHUNK, band)


def _attn_kernel(q_ref, kp_ref, kc_ref, vp_ref, vc_ref, bias_ref, o_ref, kb_ref, vb_ref, *, tq, hg):
    t = pl.program_id(1)
    pad = LOOKBACK_CHUNKS * CHUNK
    band = pad + CHUNK
    hd = ATTN_HEAD_DIM
    assert tq == pad
    kb_ref[0:pad, :] = kp_ref[...]
    kb_ref[pad:pad + tq, :] = kc_ref[...]
    vb_ref[0:pad, :] = vp_ref[...]
    vb_ref[pad:pad + tq, :] = vc_ref[...]
    kidx = lax.broadcasted_iota(jnp.int32, (CHUNK, band), 1)

    def chunk_body(ci, carry):
        r0 = pl.multiple_of(ci * CHUNK, CHUNK)
        valid = (t * tq + ci * CHUNK - pad + kidx) >= 0
        for h in range(hg):
            ls = slice(h * hd, (h + 1) * hd)
            qh = q_ref[pl.ds(r0, CHUNK), ls]
            kb = kb_ref[pl.ds(r0, band), ls]
            vb = vb_ref[pl.ds(r0, band), ls]
            s = lax.dot_general(qh, kb, (((1,), (1,)), ((), ())), preferred_element_type=F32)
            s = jnp.where(valid, s + bias_ref[h], -jnp.inf)
            p = _softmax_rows(s).astype(BF16)
            o = jnp.dot(p, vb, preferred_element_type=F32)
            o_ref[pl.ds(r0, CHUNK), ls] = o.astype(o_ref.dtype)
        return carry

    lax.fori_loop(0, tq // CHUNK, chunk_body, 0)


def _attn_core(q, k, v, bias, rows_per_batch):
    n, d = q.shape
    heads = d // ATTN_HEAD_DIM
    tq = LOOKBACK_CHUNKS * CHUNK
    nt = rows_per_batch // tq
    nbat = n // rows_per_batch
    hg = _tile(heads, 8, 1)
    gwid = hg * ATTN_HEAD_DIM
    band = tq + CHUNK
    cur = lambda b, t, g: (b * nt + t, g)
    prev = lambda b, t, g: (b * nt + jnp.maximum(t - 1, 0), g)
    vmem = 2 * (6 * _nbytes((tq, gwid), BF16) + _nbytes((hg, CHUNK, band), F32))
    vmem += 2 * _nbytes((2 * tq, gwid), BF16) + 16 * _nbytes((CHUNK, band), F32) * hg
    return pl.pallas_call(
        functools.partial(_attn_kernel, tq=tq, hg=hg),
        out_shape=jax.ShapeDtypeStruct((n, d), BF16),
        grid=(nbat, nt, heads // hg),
        in_specs=[pl.BlockSpec((tq, gwid), cur),
                  pl.BlockSpec((tq, gwid), prev),
                  pl.BlockSpec((tq, gwid), cur),
                  pl.BlockSpec((tq, gwid), prev),
                  pl.BlockSpec((tq, gwid), cur),
                  pl.BlockSpec((hg, CHUNK, band), lambda b, t, g: (g, 0, 0))],
        out_specs=pl.BlockSpec((tq, gwid), cur),
        scratch_shapes=[pltpu.VMEM((2 * tq, gwid), BF16),
                        pltpu.VMEM((2 * tq, gwid), BF16)],
        compiler_params=_params(("parallel", "parallel", "parallel"), vmem),
        name="band_attn",
    )(q, k, k, v, v, bias)


def _attn_layer(x, h, rows_per_batch, wq, wk, wv, wo, rel_table):
    q = _matmul(h, wq.astype(BF16), scale=ATTN_HEAD_DIM ** -0.5, out_dtype=BF16, name="attn_q")
    k = _matmul(h, wk.astype(BF16), out_dtype=BF16, name="attn_k")
    v = _matmul(h, wv.astype(BF16), out_dtype=BF16, name="attn_v")
    o = _attn_core(q, k, v, _rel_bias(rel_table), rows_per_batch)
    return _matmul(o, wo.astype(BF16), residual=x, out_dtype=F32, name="attn_out")


def _ffn(x, h, w_gate, w_up, w_down):
    hidden = w_gate.shape[1]
    hidden_p = -(-hidden // 1024) * 1024
    padn = ((0, 0), (0, hidden_p - hidden))
    wg = jnp.pad(w_gate.astype(BF16), padn)
    wu = jnp.pad(w_up.astype(BF16), padn)
    wd = jnp.pad(w_down.astype(BF16), ((0, hidden_p - hidden), (0, 0)))
    act = _swiglu_up(h, wg, wu)
    return _matmul(act, wd, residual=x, out_dtype=F32, bk=2816, name="ffn_down")


def kernel(x, mem, norm_mix, norm_mem_q, norm_mem_kv, norm_ffn, norm_final, mem_wq, mem_wk, mem_wv, mem_wo, ffn_w_gate, ffn_w_up, ffn_w_down, lru_w_in, lru_b_in, lru_w_gate, lru_b_gate, lru_conv_w, lru_conv_b, lru_w_a, lru_b_a, lru_w_x, lru_b_x, lru_lambda, lru_w_out, lru_b_out, ssd_w_in, ssd_conv_w, ssd_conv_b, ssd_dt_bias, ssd_a_log, ssd_d, ssd_norm, ssd_w_out, attn_wq, attn_wk, attn_wv, attn_wo, attn_rel_bias):
    nbat, seq, d = x.shape
    tm = mem.shape[1]
    depth = norm_mix.shape[0]
    xf = x.reshape(nbat * seq, d)
    memf = mem.reshape(nbat * tm, d)
    width = MEM_HEADS * MEM_HEAD_DIM
    for i in range(depth):
        kind, j = i % N_MIXERS, i // N_MIXERS
        h = _rmsnorm(xf, norm_mix[i], BF16)
        if kind == 0:
            xf = _lru_layer(xf, h, seq, lru_w_in[j], lru_b_in[j], lru_w_gate[j], lru_b_gate[j],
                            lru_conv_w[j], lru_conv_b[j], lru_w_a[j], lru_b_a[j], lru_w_x[j], lru_b_x[j],
                            lru_lambda[j], lru_w_out[j], lru_b_out[j])
        elif kind == 1:
            xf = _ssd_layer(xf, h, seq, ssd_w_in[j], ssd_conv_w[j], ssd_conv_b[j], ssd_dt_bias[j],
                            ssd_a_log[j], ssd_d[j], ssd_norm[j], ssd_w_out[j])
        else:
            xf = _attn_layer(xf, h, seq, attn_wq[j], attn_wk[j], attn_wv[j], attn_wo[j], attn_rel_bias[j])
        hm = _rmsnorm(memf, norm_mem_kv[i], BF16)
        w_kv = jnp.concatenate([mem_wk[i], mem_wv[i]], axis=1).astype(BF16)
        kv = _matmul(hm, w_kv, out_dtype=BF16, name="mem_kv").reshape(nbat, tm, 2 * width)
        xf, h2 = _mem_xattn(xf, seq, kv[:, :, :width], kv[:, :, width:], norm_mem_q[i],
                            mem_wq[i].astype(BF16), mem_wo[i].astype(BF16), norm_ffn[i])
        xf = _ffn(xf, h2, ffn_w_gate[i], ffn_w_up[i], ffn_w_down[i])
    return _rmsnorm(xf, norm_final, F32).reshape(nbat, seq, d)
```

```python
import functools

import numpy as np
import jax
import jax.numpy as jnp
from jax import lax
from jax.experimental import pallas as pl
from jax.experimental.pallas import tpu as pltpu

F32 = jnp.float32
BF16 = jnp.bfloat16

V7X_LANES = 128
V7X_SUBLANES = 8
V7X_VMEM_BYTES = 64 * 2**20
V7X_VMEM_LIMIT_CAP = V7X_VMEM_BYTES - 6 * 2**20
COMPILER_SCRATCH_ALLOWANCE = 4 * 2**20

RMS_EPS = 1e-6
N_MIXERS = 3
CHUNK = 64
LRU_C = 8.0
LRU_CONV = 4
SSD_HEAD_DIM = 64
SSD_STATE = 128
SSD_GROUPS = 8
SSD_CONV = 4
SSD_CONV_LANES = 4 * V7X_LANES
ATTN_HEAD_DIM = 128
LOOKBACK_CHUNKS = 8
MAX_REL_DIST = 256
ATTN_QCHUNKS = 4
ATTN_WINDOW = (LOOKBACK_CHUNKS + 2) * CHUNK
MEM_HEADS = 4
MEM_HEAD_DIM = 128
CONV_PAD_ROWS = V7X_SUBLANES


def _tile(dim, target, align):
    if dim <= target:
        return dim
    best = None
    for t in range(align, target + 1, align):
        if dim % t == 0:
            best = t
    assert best is not None, (dim, target, align)
    return best


def _params(semantics, vmem_bytes):
    limit = vmem_bytes + COMPILER_SCRATCH_ALLOWANCE
    return pltpu.CompilerParams(
        dimension_semantics=semantics,
        vmem_limit_bytes=int(min(max(limit, 16 * 2**20), V7X_VMEM_LIMIT_CAP)),
    )


def _nbytes(shape, dtype):
    return int(np.prod(shape)) * jnp.dtype(dtype).itemsize


def _rms(x, g):
    ms = jnp.mean(x * x, axis=-1, keepdims=True)
    return x * lax.rsqrt(ms + RMS_EPS) * g


def _softmax_rows(s):
    m = jnp.max(s, axis=-1, keepdims=True)
    e = jnp.exp(s - m)
    return e / jnp.sum(e, axis=-1, keepdims=True)


def _rmsnorm_kernel(x_ref, g_ref, o_ref):
    o_ref[...] = _rms(x_ref[...], g_ref[...]).astype(o_ref.dtype)


def _rmsnorm(x, g, out_dtype):
    n, d = x.shape
    bm = _tile(n, 256, V7X_SUBLANES)
    vmem = 2 * (_nbytes((bm, d), F32) + _nbytes((bm, d), out_dtype)) + 2 * _nbytes((bm, d), F32)
    return pl.pallas_call(
        _rmsnorm_kernel,
        out_shape=jax.ShapeDtypeStruct((n, d), out_dtype),
        grid=(n // bm,),
        in_specs=[pl.BlockSpec((bm, d), lambda i: (i, 0)),
                  pl.BlockSpec((1, d), lambda i: (0, 0))],
        out_specs=pl.BlockSpec((bm, d), lambda i: (i, 0)),
        compiler_params=_params(("parallel",), vmem),
        name="rmsnorm",
    )(x, g.reshape(1, d))


def _mm_kernel(x_ref, w_ref, *rest, has_bias, has_res, act, scale):
    rest = list(rest)
    b_ref = rest.pop(0) if has_bias else None
    r_ref = rest.pop(0) if has_res else None
    (o_ref,) = rest
    acc = jnp.dot(x_ref[...], w_ref[...], preferred_element_type=F32)
    if has_bias:
        acc = acc + b_ref[...]
    if scale is not None:
        acc = acc * scale
    if act == "gelu":
        acc = jax.nn.gelu(acc)
    if has_res:
        acc = acc + r_ref[...]
    o_ref[...] = acc.astype(o_ref.dtype)


def _matmul(x, w, *, layer=0, col0=0, ncols=None, bias=None, residual=None, act=None, scale=None,
            out_dtype=F32, bm=1024, bn=1024, name="matmul"):
    m, kdim = x.shape
    n = w.shape[2] - col0 if ncols is None else ncols
    bm = _tile(m, bm, V7X_SUBLANES * 2)
    bn = _tile(n, bn, V7X_LANES)
    assert col0 % bn == 0 and w.shape[1] == kdim
    jb0 = col0 // bn
    in_specs = [pl.BlockSpec((bm, kdim), lambda i, j: (i, 0)),
                pl.BlockSpec((None, kdim, bn), lambda i, j: (layer, 0, jb0 + j))]
    args = [x, w]
    vmem = 2 * (_nbytes((bm, kdim), x.dtype) + _nbytes((kdim, bn), w.dtype) + _nbytes((bm, bn), out_dtype))
    vmem += (2 + 2 * (act is not None)) * _nbytes((bm, bn), F32)
    if bias is not None:
        in_specs.append(pl.BlockSpec((1, bn), lambda i, j: (0, j)))
        args.append(bias.reshape(1, n).astype(F32))
    if residual is not None:
        in_specs.append(pl.BlockSpec((bm, bn), lambda i, j: (i, j)))
        args.append(residual)
        vmem += 2 * _nbytes((bm, bn), residual.dtype)
    return pl.pallas_call(
        functools.partial(_mm_kernel, has_bias=bias is not None,
                          has_res=residual is not None, act=act, scale=scale),
        out_shape=jax.ShapeDtypeStruct((m, n), out_dtype),
        grid=(m // bm, n // bn),
        in_specs=in_specs,
        out_specs=pl.BlockSpec((bm, bn), lambda i, j: (i, j)),
        compiler_params=_params(("parallel", "parallel"), vmem),
        name=name,
    )(*args)


def _swiglu_up_kernel(x_ref, wg_ref, wu_ref, o_ref, wgb_ref, wub_ref):
    @pl.when(pl.program_id(1) == 0)
    def _():
        wgb_ref[...] = wg_ref[...].astype(BF16)
        wub_ref[...] = wu_ref[...].astype(BF16)

    x = x_ref[...]
    g = jnp.dot(x, wgb_ref[...], preferred_element_type=F32)
    u = jnp.dot(x, wub_ref[...], preferred_element_type=F32)
    o_ref[...] = (jax.nn.silu(g) * u).astype(o_ref.dtype)


def _swiglu_up(x, wg, wu, layer, *, bm=1024, bn=256):
    m, kdim = x.shape
    n = wg.shape[2]
    bm = _tile(m, bm, V7X_SUBLANES * 2)
    bn = _tile(n, bn, V7X_LANES)
    vmem = 2 * (_nbytes((bm, kdim), BF16) + 2 * _nbytes((kdim, bn), F32) + _nbytes((bm, bn), BF16))
    vmem += 2 * _nbytes((kdim, bn), BF16) + 5 * _nbytes((bm, bn), F32) + 2 * _nbytes((kdim, bn), F32)
    wspec = pl.BlockSpec((None, kdim, bn), lambda j, i: (layer, 0, j))
    return pl.pallas_call(
        _swiglu_up_kernel,
        out_shape=jax.ShapeDtypeStruct((m, n), BF16),
        grid=(n // bn, m // bm),
        in_specs=[pl.BlockSpec((bm, kdim), lambda j, i: (i, 0)), wspec, wspec],
        out_specs=pl.BlockSpec((bm, bn), lambda j, i: (i, j)),
        scratch_shapes=[pltpu.VMEM((kdim, bn), BF16), pltpu.VMEM((kdim, bn), BF16)],
        compiler_params=_params(("parallel", "arbitrary"), vmem),
        name="swiglu_up",
    )(x, wg, wu)


def _memattn_kernel(x_ref, gq_ref, wq_ref, k_ref, v_ref, wo_ref, gf_ref, xo_ref, ho_ref):
    x = x_ref[...]
    h = _rms(x, gq_ref[...]).astype(BF16)
    q = jnp.dot(h, wq_ref[...], preferred_element_type=F32).astype(BF16)
    k = k_ref[0]
    v = v_ref[0]
    scale = MEM_HEAD_DIM ** -0.5
    outs = []
    for hh in range(MEM_HEADS):
        sl = slice(hh * MEM_HEAD_DIM, (hh + 1) * MEM_HEAD_DIM)
        s = lax.dot_general(q[:, sl], k[:, sl], (((1,), (1,)), ((), ())),
                            preferred_element_type=F32) * scale
        p = _softmax_rows(s).astype(BF16)
        outs.append(jnp.dot(p, v[:, sl], preferred_element_type=F32).astype(BF16))
    o = jnp.concatenate(outs, axis=1)
    x2 = x + jnp.dot(o, wo_ref[...], preferred_element_type=F32)
    xo_ref[...] = x2
    ho_ref[...] = _rms(x2, gf_ref[...]).astype(BF16)


def _mem_xattn(x, rows_per_batch, kv, g_q, wq, wo, layer, g_ffn):
    n, d = x.shape
    nb, tm, width2 = kv.shape
    width = width2 // 2
    bm = _tile(rows_per_batch, 256, V7X_SUBLANES * 2)
    per_batch = rows_per_batch // bm
    vmem = 2 * (2 * _nbytes((bm, d), F32) + _nbytes((bm, d), BF16) + 2 * _nbytes((d, width), BF16)
                + 2 * _nbytes((tm, width), BF16))
    vmem += 4 * _nbytes((bm, d), F32)
    return pl.pallas_call(
        _memattn_kernel,
        out_shape=(jax.ShapeDtypeStruct((n, d), F32), jax.ShapeDtypeStruct((n, d), BF16)),
        grid=(n // bm,),
        in_specs=[pl.BlockSpec((bm, d), lambda i: (i, 0)),
                  pl.BlockSpec((1, d), lambda i: (0, 0)),
                  pl.BlockSpec((None, d, width), lambda i: (layer, 0, 0)),
                  pl.BlockSpec((1, tm, width), lambda i: (i // per_batch, 0, 0)),
                  pl.BlockSpec((1, tm, width), lambda i: (i // per_batch, 0, 1)),
                  pl.BlockSpec((None, width, d), lambda i: (layer, 0, 0)),
                  pl.BlockSpec((1, d), lambda i: (0, 0))],
        out_specs=(pl.BlockSpec((bm, d), lambda i: (i, 0)),
                   pl.BlockSpec((bm, d), lambda i: (i, 0))),
        compiler_params=_params(("parallel",), vmem),
        name="mem_xattn",
    )(x, g_q.reshape(1, d), wq, kv, kv, wo, g_ffn.reshape(1, d))


def _lru_kernel(xin_ref, gate_ref, cw_ref, cb_ref, wax_ref, bax_ref, lam_ref, o_ref,
                xs_ref, a_ref, u_ref, h_ref, *, tb, nblk, blk):
    t = pl.program_id(2)
    pad = CONV_PAD_ROWS

    @pl.when(t == 0)
    def _():
        xs_ref[0:pad, :] = jnp.zeros((pad, xs_ref.shape[1]), F32)
        h_ref[...] = jnp.zeros(h_ref.shape, F32)

    @pl.when(t > 0)
    def _():
        xs_ref[0:pad, :] = xs_ref[tb:tb + pad, :]

    xs_ref[pad:pad + tb, :] = xin_ref[...]
    cw = cw_ref[...]
    xc = cb_ref[...]
    for j in range(LRU_CONV):
        off = pad - (LRU_CONV - 1) + j
        xc = xc + cw[j:j + 1, :] * xs_ref[off:off + tb, :]

    neg_sp = -LRU_C * jax.nn.softplus(-lam_ref[...])
    for nb in range(nblk):
        sl = slice(nb * blk, (nb + 1) * blk)
        xcb = xc[:, sl]
        gates = jnp.dot(xcb.astype(BF16), wax_ref[nb], preferred_element_type=F32) + bax_ref[nb]
        r = jax.nn.sigmoid(gates[:, :blk])
        ig = jax.nn.sigmoid(gates[:, blk:])
        log_a = neg_sp[:, sl] * r
        th = jnp.tanh(log_a)
        one_minus_a2 = -2.0 * th / (1.0 - th)
        a_ref[:, sl] = jnp.exp(log_a)
        u_ref[:, sl] = jnp.sqrt(one_minus_a2) * (ig * xcb)

    width = a_ref.shape[1]
    row = lax.broadcasted_iota(jnp.int32, (V7X_SUBLANES, width), 0)

    def scan_rows(i, h):
        r0 = pl.multiple_of(i * V7X_SUBLANES, V7X_SUBLANES)
        a = a_ref[pl.ds(r0, V7X_SUBLANES), :]
        u = u_ref[pl.ds(r0, V7X_SUBLANES), :]
        for sh in (1, 2, 4):
            a_prev = jnp.where(row >= sh, pltpu.roll(a, sh, axis=0), 1.0)
            u_prev = jnp.where(row >= sh, pltpu.roll(u, sh, axis=0), 0.0)
            u = a * u_prev + u
            a = a * a_prev
        hs = u + a * h
        u_ref[pl.ds(r0, V7X_SUBLANES), :] = hs
        return hs[V7X_SUBLANES - 1:V7X_SUBLANES, :]

    h_ref[...] = lax.fori_loop(0, tb // V7X_SUBLANES, scan_rows, h_ref[...])
    o_ref[...] = (u_ref[...] * gate_ref[...].astype(F32)).astype(o_ref.dtype)


def _lru_core(xin, gate, rows_per_batch, conv_w, conv_b, w_ax, b_ax, lam):
    n, r = xin.shape
    nblocks, blk = w_ax.shape[0], w_ax.shape[1]
    nblk = _tile(nblocks, 3, 1)
    c = nblk * blk
    tb = _tile(rows_per_batch, 512, V7X_SUBLANES * 2)
    nt = rows_per_batch // tb
    nbat = n // rows_per_batch
    pad = CONV_PAD_ROWS
    vmem = 2 * (_nbytes((tb, c), F32) + 2 * _nbytes((tb, c), BF16) + _nbytes((nblk, blk, 2 * blk), BF16))
    vmem += _nbytes((tb + pad, c), F32) + 2 * _nbytes((tb, c), F32) + 6 * _nbytes((tb, c), F32)
    return pl.pallas_call(
        functools.partial(_lru_kernel, tb=tb, nblk=nblk, blk=blk),
        out_shape=jax.ShapeDtypeStruct((n, r), BF16),
        grid=(nbat, r // c, nt),
        in_specs=[pl.BlockSpec((tb, c), lambda b, j, t: (b * nt + t, j)),
                  pl.BlockSpec((tb, c), lambda b, j, t: (b * nt + t, j)),
                  pl.BlockSpec((LRU_CONV, c), lambda b, j, t: (0, j)),
                  pl.BlockSpec((1, c), lambda b, j, t: (0, j)),
                  pl.BlockSpec((nblk, blk, 2 * blk), lambda b, j, t: (j, 0, 0)),
                  pl.BlockSpec((nblk, 1, 2 * blk), lambda b, j, t: (j, 0, 0)),
                  pl.BlockSpec((1, c), lambda b, j, t: (0, j))],
        out_specs=pl.BlockSpec((tb, c), lambda b, j, t: (b * nt + t, j)),
        scratch_shapes=[pltpu.VMEM((tb + pad, c), F32),
                        pltpu.VMEM((tb, c), F32),
                        pltpu.VMEM((tb, c), F32),
                        pltpu.VMEM((1, c), F32)],
        compiler_params=_params(("parallel", "parallel", "arbitrary"), vmem),
        name="lru_core",
    )(xin, gate, conv_w, conv_b.reshape(1, r), w_ax, b_ax, lam.reshape(1, r))


def _lru_layer(x, h, rows_per_batch, layer, w_in, b_in, w_gate, b_gate, conv_w, conv_b,
               w_a, b_a, w_x, b_x, lam, w_out, b_out):
    nblocks, blk = w_a.shape[0], w_a.shape[1]
    gate = _matmul(h, w_gate, layer=layer, bias=b_gate, act="gelu", out_dtype=BF16,
                   bn=768, name="lru_gate")
    xin = _matmul(h, w_in, layer=layer, bias=b_in, out_dtype=F32, bn=768, name="lru_in")
    w_ax = jnp.concatenate([w_a, w_x], axis=-1).astype(BF16)
    b_ax = jnp.concatenate([b_a.reshape(nblocks, 1, blk), b_x.reshape(nblocks, 1, blk)], axis=-1)
    hs = _lru_core(xin, gate, rows_per_batch, conv_w, conv_b, w_ax, b_ax, lam)
    return _matmul(hs, w_out, layer=layer, bias=b_out, residual=x, out_dtype=F32,
                   bm=512, name="lru_out")


def _split3(v):
    hi = v.astype(BF16)
    r1 = v - hi.astype(F32)
    mid = r1.astype(BF16)
    lo = (r1 - mid.astype(F32)).astype(BF16)
    return hi, mid, lo


def _ssd_kernel(z_ref, xbc_ref, dtr_ref, cw_ref, cb_ref, dtb_ref, alog_ref, dsk_ref, nw_ref, y_ref,
                xs_ref, xa_ref, st_ref, ct2_ref, dt2_ref, wt2_ref, *, d_inner):
    c = pl.program_id(1)
    L = CHUNK
    pad = CONV_PAD_ROWS
    lanes = V7X_LANES
    half = lanes // 2
    gw = d_inner // SSD_GROUPS
    pairs_per_group = gw // lanes
    npair = d_inner // lanes
    assert SSD_HEAD_DIM == half and SSD_STATE == lanes and 2 * npair == lanes and L == half

    @pl.when(c == 0)
    def _():
        xs_ref[0:pad, :] = jnp.zeros((pad, xs_ref.shape[1]), F32)
        st_ref[...] = jnp.zeros(st_ref.shape, F32)

    @pl.when(c > 0)
    def _():
        xs_ref[0:pad, :] = xs_ref[L:L + pad, :]

    xs_ref[pad:pad + L, :] = xbc_ref[...]
    cwid = SSD_CONV_LANES

    def conv_body(i, carry):
        l0 = pl.multiple_of(i * cwid, cwid)
        cw = cw_ref[:, pl.ds(l0, cwid)]
        xc = cb_ref[:, pl.ds(l0, cwid)]
        for j in range(SSD_CONV):
            off = pad - (SSD_CONV - 1) + j
            xc = xc + cw[j:j + 1, :] * xs_ref[off:off + L, pl.ds(l0, cwid)]
        xa_ref[:, pl.ds(l0, cwid)] = jax.nn.silu(xc)
        return carry

    lax.fori_loop(0, xs_ref.shape[1] // cwid, conv_body, 0)

    dt = jax.nn.softplus(dtr_ref[...] + dtb_ref[...])
    da = dt * (-jnp.exp(alog_ref[...]))
    ri = lax.broadcasted_iota(jnp.int32, (L, L), 0)
    ci = lax.broadcasted_iota(jnp.int32, (L, L), 1)
    tril = jnp.where(ri >= ci, 1.0, 0.0).astype(BF16)
    parts = jnp.dot(tril, jnp.concatenate(_split3(da), axis=1), preferred_element_type=F32)
    cum = (parts[:, :lanes] + parts[:, lanes:2 * lanes]) + parts[:, 2 * lanes:]
    last = cum[L - 1:L, :]
    w_end = dt * jnp.exp(last - cum)
    dec_chunk = jnp.exp(last)

    def pair_rows(v):
        vt = v.T
        return jnp.concatenate([vt[:npair], vt[npair:]], axis=1)

    ct2_ref[...] = pair_rows(cum)
    dt2_ref[...] = pair_rows(dt)
    wt2_ref[...] = pair_rows(w_end)

    lane_i = lax.broadcasted_iota(jnp.int32, (L, lanes), 1)
    row_i = lax.broadcasted_iota(jnp.int32, (L, lanes), 0)
    lo_half = lane_i < half
    causal2 = row_i >= jnp.where(lo_half, lane_i, lane_i - half)
    lo_half_row = lax.broadcasted_iota(jnp.int32, (1, lanes), 1) < half
    lo_half_sq = lax.broadcasted_iota(jnp.int32, (lanes, lanes), 1) < half
    top_half_sq = lax.broadcasted_iota(jnp.int32, (lanes, lanes), 0) < half
    blockdiag = lo_half_sq == top_half_sq

    def group_body(g, carry):
        x0 = pl.multiple_of(g * gw, gw)
        b0 = pl.multiple_of(d_inner + g * SSD_STATE, SSD_STATE)
        c0 = pl.multiple_of(d_inner + SSD_GROUPS * SSD_STATE + g * SSD_STATE, SSD_STATE)
        p0 = pl.multiple_of(g * pairs_per_group, pairs_per_group)
        bg = xa_ref[:, pl.ds(b0, SSD_STATE)]
        cg = xa_ref[:, pl.ds(c0, SSD_STATE)].astype(BF16)
        b2 = jnp.concatenate([bg, bg], axis=0)
        cb2 = lax.dot_general(cg, b2.astype(BF16), (((1,), (1,)), ((), ())),
                              preferred_element_type=F32)
        bt2 = b2.T
        sg = st_ref[g]
        yoff = jnp.dot(cg, sg.astype(BF16), preferred_element_type=F32)
        shift = lax.rem(lanes - g * pairs_per_group, lanes)
        cum_r = pltpu.roll(cum, shift, axis=1)
        dec_r = pltpu.roll(dec_chunk, shift, axis=1)
        ct8 = ct2_ref[pl.ds(p0, pairs_per_group), :]
        dt8 = dt2_ref[pl.ds(p0, pairs_per_group), :]
        wt8 = wt2_ref[pl.ds(p0, pairs_per_group), :]
        xg = xa_ref[:, pl.ds(x0, gw)]
        ys, snew = [], []
        for pj in range(pairs_per_group):
            ls = slice(pj * lanes, (pj + 1) * lanes)
            xp = xg[:, ls]
            colp = jnp.where(lo_half,
                             jnp.broadcast_to(cum_r[:, pj:pj + 1], (L, lanes)),
                             jnp.broadcast_to(cum_r[:, half + pj:half + pj + 1], (L, lanes)))
            seg = jnp.where(causal2, colp - ct8[pj:pj + 1, :], -jnp.inf)
            m_diag = (cb2 * jnp.exp(seg) * dt8[pj:pj + 1, :]).astype(BF16)
            m_state = (bt2 * wt8[pj:pj + 1, :]).astype(BF16)
            x2 = jnp.concatenate([xp, xp], axis=0)
            xbd = jnp.where(blockdiag, x2, 0.0).astype(BF16)
            res = jnp.dot(jnp.concatenate([m_diag, m_state], axis=0), xbd,
                          preferred_element_type=F32)
            ys.append(res[:L] + jnp.exp(colp) * yoff[:, ls])
            decp = jnp.where(lo_half_row,
                             jnp.broadcast_to(dec_r[:, pj:pj + 1], (1, lanes)),
                             jnp.broadcast_to(dec_r[:, half + pj:half + pj + 1], (1, lanes)))
            snew.append(decp * sg[:, ls] + res[L:])
        st_ref[g] = jnp.concatenate(snew, axis=1)
        y = jnp.concatenate(ys, axis=1) + dsk_ref[:, pl.ds(x0, gw)] * xg
        y = y * jax.nn.silu(z_ref[:, pl.ds(x0, gw)].astype(F32))
        y = _rms(y, nw_ref[:, pl.ds(x0, gw)])
        y_ref[:, pl.ds(x0, gw)] = y.astype(y_ref.dtype)
        return carry

    lax.fori_loop(0, SSD_GROUPS, group_body, 0)


def _ssd_core(z, xbc, dtr, rows_per_batch, conv_w, conv_b, dt_bias, a_log, d_rep, norm_w):
    n, d_inner = z.shape
    cdim = xbc.shape[1]
    nheads = dtr.shape[1]
    nc = rows_per_batch // CHUNK
    nbat = n // rows_per_batch
    gw = d_inner // SSD_GROUPS
    npair = d_inner // V7X_LANES
    pad = CONV_PAD_ROWS
    assert cdim % SSD_CONV_LANES == 0
    vmem = 2 * (2 * _nbytes((CHUNK, d_inner), BF16) + _nbytes((CHUNK, cdim), F32)
                + _nbytes((SSD_CONV + 1, cdim), F32))
    vmem += _nbytes((CHUNK + pad, cdim), F32) + _nbytes((CHUNK, cdim), F32)
    vmem += _nbytes((SSD_GROUPS, SSD_STATE, gw), F32) + 8 * _nbytes((SSD_STATE, gw), F32)
    row = lambda b, c: (b * nc + c, 0)
    const = lambda b, c: (0, 0)
    return pl.pallas_call(
        functools.partial(_ssd_kernel, d_inner=d_inner),
        out_shape=jax.ShapeDtypeStruct((n, d_inner), BF16),
        grid=(nbat, nc),
        in_specs=[pl.BlockSpec((CHUNK, d_inner), row),
                  pl.BlockSpec((CHUNK, cdim), row),
                  pl.BlockSpec((CHUNK, nheads), row),
                  pl.BlockSpec((SSD_CONV, cdim), const),
                  pl.BlockSpec((1, cdim), const),
                  pl.BlockSpec((1, nheads), const),
                  pl.BlockSpec((1, nheads), const),
                  pl.BlockSpec((1, d_inner), const),
                  pl.BlockSpec((1, d_inner), const)],
        out_specs=pl.BlockSpec((CHUNK, d_inner), row),
        scratch_shapes=[pltpu.VMEM((CHUNK + pad, cdim), F32),
                        pltpu.VMEM((CHUNK, cdim), F32),
                        pltpu.VMEM((SSD_GROUPS, SSD_STATE, gw), F32),
                        pltpu.VMEM((npair, V7X_LANES), F32),
                        pltpu.VMEM((npair, V7X_LANES), F32),
                        pltpu.VMEM((npair, V7X_LANES), F32)],
        compiler_params=_params(("parallel", "arbitrary"), vmem),
        name="ssd_core",
    )(z, xbc, dtr, conv_w, conv_b.reshape(1, cdim), dt_bias.reshape(1, nheads),
      a_log.reshape(1, nheads), d_rep.reshape(1, d_inner), norm_w.reshape(1, d_inner))


def _ssd_layer(x, h, rows_per_batch, layer, w_in, w_in_f32, conv_w, conv_b, dt_bias, a_log, d_skip,
               norm_w, w_out):
    nheads = dt_bias.shape[0]
    d_inner = nheads * SSD_HEAD_DIM
    cdim = conv_w.shape[1]
    perm = np.concatenate([np.arange(0, nheads, 2), np.arange(1, nheads, 2)])
    w_dt = w_in_f32[:, d_inner + cdim:][:, perm].astype(BF16)[None]
    z = _matmul(h, w_in, layer=layer, ncols=d_inner, out_dtype=BF16, name="ssd_in_z")
    xbc = _matmul(h, w_in, layer=layer, col0=d_inner, ncols=cdim, out_dtype=F32, name="ssd_in_xbc")
    dtr = _matmul(h, w_dt, out_dtype=F32, name="ssd_in_dt")
    d_rep = jnp.repeat(d_skip.astype(F32), SSD_HEAD_DIM)
    y = _ssd_core(z, xbc, dtr, rows_per_batch, conv_w, conv_b, dt_bias[perm], a_log[perm], d_rep, norm_w)
    return _matmul(y, w_out, layer=layer, residual=x, out_dtype=F32, bm=512, bn=512, name="ssd_out")


def _bias_gather_kernel(idx_ref, tab_ref, o_ref, *, heads):
    idx = idx_ref[...]
    ntab = tab_ref.shape[1]
    rows = lax.broadcasted_iota(jnp.int32, (ntab, idx.shape[1]), 0)
    onehot = jnp.where(rows == idx, 1.0, 0.0).astype(BF16)
    r = jnp.dot(tab_ref[...], onehot, preferred_element_type=F32)
    o_ref[...] = (r[0:heads] + r[heads:2 * heads]) + r[2 * heads:3 * heads]


def _rel_bias(rel_table):
    heads, ntab = rel_table.shape
    pad = LOOKBACK_CHUNKS * CHUNK
    band = pad + CHUNK
    rel = (np.arange(CHUNK)[:, None] + pad) - np.arange(band)[None, :]
    idx = (np.clip(rel, -MAX_REL_DIST, MAX_REL_DIST) + MAX_REL_DIST).astype(np.int32).reshape(1, -1)
    npos = idx.shape[1]
    ntab_p = -(-ntab // V7X_LANES) * V7X_LANES
    tab = jnp.pad(rel_table.astype(F32), ((0, 0), (0, ntab_p - ntab)))
    tab3 = jnp.concatenate(_split3(tab), axis=0)
    tp = _tile(npos, 2048, V7X_LANES)
    out = pl.pallas_call(
        functools.partial(_bias_gather_kernel, heads=heads),
        out_shape=jax.ShapeDtypeStruct((heads, npos), F32),
        grid=(npos // tp,),
        in_specs=[pl.BlockSpec((1, tp), lambda i: (0, i)),
                  pl.BlockSpec((3 * heads, ntab_p), lambda i: (0, 0))],
        out_specs=pl.BlockSpec((heads, tp), lambda i: (0, i)),
        compiler_params=_params(("parallel",), 16 * 2**20),
        name="rel_bias_gather",
    )(jnp.asarray(idx), tab3)
    bias = out.reshape(heads, CHUNK, band)
    masked = jnp.full((heads, CHUNK, CHUNK), -jnp.inf, F32)
    return jnp.stack([jnp.concatenate([bias, masked], axis=2),
                      jnp.concatenate([masked, bias], axis=2)], axis=1)


def _attn_kernel(q_ref, kp_ref, kc_ref, vp_ref, vc_ref, bias_ref, o_ref, kb_ref, vb_ref, *, tq, hg):
    t = pl.program_id(1)
    pad = LOOKBACK_CHUNKS * CHUNK
    hd = ATTN_HEAD_DIM
    qb = ATTN_QCHUNKS * CHUNK
    kbw = pad + qb
    assert tq == pad and ATTN_QCHUNKS % 2 == 0 and ATTN_WINDOW == pad + 2 * CHUNK
    kb_ref[0:pad, :] = kp_ref[...]
    kb_ref[pad:pad + tq, :] = kc_ref[...]
    vb_ref[0:pad, :] = vp_ref[...]
    vb_ref[pad:pad + tq, :] = vc_ref[...]
    col = lax.broadcasted_iota(jnp.int32, (CHUNK, ATTN_WINDOW), 1)

    def block_body(sb, carry):
        r0 = pl.multiple_of(sb * qb, qb)
        first_key = t * tq + sb * qb - pad
        for h in range(hg):
            ls = slice(h * hd, (h + 1) * hd)
            qh = q_ref[pl.ds(r0, qb), ls]
            kb = kb_ref[pl.ds(r0, kbw), ls]
            vb = vb_ref[pl.ds(r0, kbw), ls]
            s = lax.dot_general(qh, kb, (((1,), (1,)), ((), ())), preferred_element_type=F32)
            prow = []
            for c in range(ATTN_QCHUNKS):
                w0 = (c // 2) * 2 * CHUNK
                sc = s[c * CHUNK:(c + 1) * CHUNK, w0:w0 + ATTN_WINDOW] + bias_ref[h, c % 2]
                sc = jnp.where(first_key + w0 + col >= 0, sc, -jnp.inf)
                p = _softmax_rows(sc).astype(BF16)
                parts = [p]
                if w0 > 0:
                    parts.insert(0, jnp.zeros((CHUNK, w0), BF16))
                if w0 + ATTN_WINDOW < kbw:
                    parts.append(jnp.zeros((CHUNK, kbw - w0 - ATTN_WINDOW), BF16))
                prow.append(jnp.concatenate(parts, axis=1))
            o = jnp.dot(jnp.concatenate(prow, axis=0), vb, preferred_element_type=F32)
            o_ref[pl.ds(r0, qb), ls] = o.astype(o_ref.dtype)
        return carry

    lax.fori_loop(0, tq // qb, block_body, 0)


def _attn_core(q, k, v, bias, rows_per_batch):
    n, d = q.shape
    heads = d // ATTN_HEAD_DIM
    tq = LOOKBACK_CHUNKS * CHUNK
    nt = rows_per_batch // tq
    nbat = n // rows_per_batch
    hg = _tile(heads, 8, 1)
    gwid = hg * ATTN_HEAD_DIM
    kbw = tq + ATTN_QCHUNKS * CHUNK
    cur = lambda b, t, g: (b * nt + t, g)
    prev = lambda b, t, g: (b * nt + jnp.maximum(t - 1, 0), g)
    vmem = 2 * (6 * _nbytes((tq, gwid), BF16) + _nbytes((hg, 2, CHUNK, ATTN_WINDOW), F32))
    vmem += 2 * _nbytes((2 * tq, gwid), BF16) + 4 * hg * _nbytes((ATTN_QCHUNKS * CHUNK, kbw), F32)
    return pl.pallas_call(
        functools.partial(_attn_kernel, tq=tq, hg=hg),
        out_shape=jax.ShapeDtypeStruct((n, d), BF16),
        grid=(nbat, nt, heads // hg),
        in_specs=[pl.BlockSpec((tq, gwid), cur),
                  pl.BlockSpec((tq, gwid), prev),
                  pl.BlockSpec((tq, gwid), cur),
                  pl.BlockSpec((tq, gwid), prev),
                  pl.BlockSpec((tq, gwid), cur),
                  pl.BlockSpec((hg, 2, CHUNK, ATTN_WINDOW), lambda b, t, g: (g, 0, 0, 0))],
        out_specs=pl.BlockSpec((tq, gwid), cur),
        scratch_shapes=[pltpu.VMEM((2 * tq, gwid), BF16),
                        pltpu.VMEM((2 * tq, gwid), BF16)],
        compiler_params=_params(("parallel", "parallel", "parallel"), vmem),
        name="band_attn",
    )(q, k, k, v, v, bias)


def _attn_layer(x, h, rows_per_batch, layer, wq, wk, wv, wo, rel_table):
    q = _matmul(h, wq, layer=layer, scale=ATTN_HEAD_DIM ** -0.5, out_dtype=BF16, name="attn_q")
    k = _matmul(h, wk, layer=layer, out_dtype=BF16, name="attn_k")
    v = _matmul(h, wv, layer=layer, out_dtype=BF16, name="attn_v")
    o = _attn_core(q, k, v, _rel_bias(rel_table), rows_per_batch)
    return _matmul(o, wo, layer=layer, residual=x, out_dtype=F32, name="attn_out")


def _ffn(x, h, layer, w_gate, w_up, w_down):
    act = _swiglu_up(h, w_gate, w_up, layer)
    return _matmul(act, w_down, layer=layer, residual=x, out_dtype=F32, bm=512, bn=512, name="ffn_down")


def kernel(x, mem, norm_mix, norm_mem_q, norm_mem_kv, norm_ffn, norm_final, mem_wq, mem_wk, mem_wv, mem_wo, ffn_w_gate, ffn_w_up, ffn_w_down, lru_w_in, lru_b_in, lru_w_gate, lru_b_gate, lru_conv_w, lru_conv_b, lru_w_a, lru_b_a, lru_w_x, lru_b_x, lru_lambda, lru_w_out, lru_b_out, ssd_w_in, ssd_conv_w, ssd_conv_b, ssd_dt_bias, ssd_a_log, ssd_d, ssd_norm, ssd_w_out, attn_wq, attn_wk, attn_wv, attn_wo, attn_rel_bias):
    nbat, seq, d = x.shape
    tm = mem.shape[1]
    depth = norm_mix.shape[0]
    xf = x.reshape(nbat * seq, d)
    memf = mem.reshape(nbat * tm, d)
    bf = lambda w: w.astype(BF16)
    lru_wi, lru_wg, lru_wo = bf(lru_w_in), bf(lru_w_gate), bf(lru_w_out)
    ssd_wi, ssd_wo = bf(ssd_w_in), bf(ssd_w_out)
    a_wq, a_wk, a_wv, a_wo = bf(attn_wq), bf(attn_wk), bf(attn_wv), bf(attn_wo)
    m_wq, m_wo = bf(mem_wq), bf(mem_wo)
    m_wkv = bf(jnp.concatenate([mem_wk, mem_wv], axis=2))
    ffn_wd = bf(ffn_w_down)
    for i in range(depth):
        kind, j = i % N_MIXERS, i // N_MIXERS
        h = _rmsnorm(xf, norm_mix[i], BF16)
        if kind == 0:
            xf = _lru_layer(xf, h, seq, j, lru_wi, lru_b_in[j], lru_wg, lru_b_gate[j],
                            lru_conv_w[j], lru_conv_b[j], lru_w_a[j], lru_b_a[j], lru_w_x[j], lru_b_x[j],
                            lru_lambda[j], lru_wo, lru_b_out[j])
        elif kind == 1:
            xf = _ssd_layer(xf, h, seq, j, ssd_wi, ssd_w_in[j], ssd_conv_w[j], ssd_conv_b[j], ssd_dt_bias[j],
                            ssd_a_log[j], ssd_d[j], ssd_norm[j], ssd_wo)
        else:
            xf = _attn_layer(xf, h, seq, j, a_wq, a_wk, a_wv, a_wo, attn_rel_bias[j])
        hm = _rmsnorm(memf, norm_mem_kv[i], BF16)
        kv = _matmul(hm, m_wkv, layer=i, out_dtype=BF16, name="mem_kv").reshape(nbat, tm, -1)
        xf, h2 = _mem_xattn(xf, seq, kv, norm_mem_q[i], m_wq, m_wo, i, norm_ffn[i])
        xf = _ffn(xf, h2, i, ffn_w_gate, ffn_w_up, ffn_wd)
    return _rmsnorm(xf, norm_final, F32).reshape(nbat, seq, d)
```

```python
import functools

import numpy as np
import jax
import jax.numpy as jnp
from jax import lax
from jax.experimental import pallas as pl
from jax.experimental.pallas import tpu as pltpu

F32 = jnp.float32
BF16 = jnp.bfloat16

V7X_LANES = 128
V7X_SUBLANES = 8
V7X_VMEM_BYTES = 64 * 2**20
V7X_VMEM_LIMIT_CAP = V7X_VMEM_BYTES - 6 * 2**20
COMPILER_SCRATCH_ALLOWANCE = 4 * 2**20

RMS_EPS = 1e-6
N_MIXERS = 3
CHUNK = 64
LRU_C = 8.0
LRU_CONV = 4
SSD_HEAD_DIM = 64
SSD_STATE = 128
SSD_GROUPS = 8
SSD_CONV = 4
SSD_CONV_LANES = 4 * V7X_LANES
ATTN_HEAD_DIM = 128
LOOKBACK_CHUNKS = 8
MAX_REL_DIST = 256
ATTN_QCHUNKS = 4
ATTN_WINDOW = (LOOKBACK_CHUNKS + 2) * CHUNK
MEM_HEADS = 4
MEM_HEAD_DIM = 128
CONV_PAD_ROWS = V7X_SUBLANES
ROW_BLOCK = V7X_SUBLANES * V7X_SUBLANES


def _tile(dim, target, align):
    if dim <= target:
        return dim
    best = None
    for t in range(align, target + 1, align):
        if dim % t == 0:
            best = t
    assert best is not None, (dim, target, align)
    return best


def _params(semantics, vmem_bytes):
    limit = vmem_bytes + COMPILER_SCRATCH_ALLOWANCE
    return pltpu.CompilerParams(
        dimension_semantics=semantics,
        vmem_limit_bytes=int(min(max(limit, 16 * 2**20), V7X_VMEM_LIMIT_CAP)),
    )


def _nbytes(shape, dtype):
    return int(np.prod(shape)) * jnp.dtype(dtype).itemsize


def _rms(x, g):
    ms = jnp.mean(x * x, axis=-1, keepdims=True)
    return x * lax.rsqrt(ms + RMS_EPS) * g


def _softmax_rows(s):
    m = jnp.max(s, axis=-1, keepdims=True)
    e = jnp.exp(s - m)
    return e / jnp.sum(e, axis=-1, keepdims=True)


def _rmsnorm_kernel(x_ref, g_ref, o_ref):
    o_ref[...] = _rms(x_ref[...], g_ref[...]).astype(o_ref.dtype)


def _rmsnorm(x, g, out_dtype):
    n, d = x.shape
    bm = _tile(n, 256, V7X_SUBLANES)
    vmem = 2 * (_nbytes((bm, d), F32) + _nbytes((bm, d), out_dtype)) + 2 * _nbytes((bm, d), F32)
    return pl.pallas_call(
        _rmsnorm_kernel,
        out_shape=jax.ShapeDtypeStruct((n, d), out_dtype),
        grid=(n // bm,),
        in_specs=[pl.BlockSpec((bm, d), lambda i: (i, 0)),
                  pl.BlockSpec((1, d), lambda i: (0, 0))],
        out_specs=pl.BlockSpec((bm, d), lambda i: (i, 0)),
        compiler_params=_params(("parallel",), vmem),
        name="rmsnorm",
    )(x, g.reshape(1, d))


def _mm_kernel(x_ref, w_ref, *rest, has_bias, has_res, act, scale):
    rest = list(rest)
    b_ref = rest.pop(0) if has_bias else None
    r_ref = rest.pop(0) if has_res else None
    (o_ref,) = rest
    acc = jnp.dot(x_ref[...], w_ref[...], preferred_element_type=F32)
    if has_bias:
        acc = acc + b_ref[...]
    if scale is not None:
        acc = acc * scale
    if act == "gelu":
        acc = jax.nn.gelu(acc)
    if has_res:
        acc = acc + r_ref[...]
    o_ref[...] = acc.astype(o_ref.dtype)


def _matmul(x, w, *, layer=0, col0=0, ncols=None, bias=None, residual=None, act=None, scale=None,
            out_dtype=F32, bm=1024, bn=1024, name="matmul"):
    m, kdim = x.shape
    n = w.shape[2] - col0 if ncols is None else ncols
    bm = _tile(m, bm, V7X_SUBLANES * 2)
    bn = _tile(n, bn, V7X_LANES)
    assert col0 % bn == 0 and w.shape[1] == kdim
    jb0 = col0 // bn
    in_specs = [pl.BlockSpec((bm, kdim), lambda i, j: (i, 0)),
                pl.BlockSpec((None, kdim, bn), lambda i, j: (layer, 0, jb0 + j))]
    args = [x, w]
    vmem = 2 * (_nbytes((bm, kdim), x.dtype) + _nbytes((kdim, bn), w.dtype) + _nbytes((bm, bn), out_dtype))
    vmem += (2 + 2 * (act is not None)) * _nbytes((bm, bn), F32)
    if bias is not None:
        in_specs.append(pl.BlockSpec((1, bn), lambda i, j: (0, j)))
        args.append(bias.reshape(1, n).astype(F32))
    if residual is not None:
        in_specs.append(pl.BlockSpec((bm, bn), lambda i, j: (i, j)))
        args.append(residual)
        vmem += 2 * _nbytes((bm, bn), residual.dtype)
    return pl.pallas_call(
        functools.partial(_mm_kernel, has_bias=bias is not None,
                          has_res=residual is not None, act=act, scale=scale),
        out_shape=jax.ShapeDtypeStruct((m, n), out_dtype),
        grid=(m // bm, n // bn),
        in_specs=in_specs,
        out_specs=pl.BlockSpec((bm, bn), lambda i, j: (i, j)),
        compiler_params=_params(("parallel", "parallel"), vmem),
        name=name,
    )(*args)


def _swiglu_up_kernel(x_ref, wg_ref, wu_ref, o_ref, wgb_ref, wub_ref):
    @pl.when(pl.program_id(1) == 0)
    def _():
        wgb_ref[...] = wg_ref[...].astype(BF16)
        wub_ref[...] = wu_ref[...].astype(BF16)

    x = x_ref[...]
    g = jnp.dot(x, wgb_ref[...], preferred_element_type=F32)
    u = jnp.dot(x, wub_ref[...], preferred_element_type=F32)
    o_ref[...] = (jax.nn.silu(g) * u).astype(o_ref.dtype)


def _swiglu_up(x, wg, wu, layer, *, bm=1024, bn=256):
    m, kdim = x.shape
    n = wg.shape[2]
    bm = _tile(m, bm, V7X_SUBLANES * 2)
    bn = _tile(n, bn, V7X_LANES)
    vmem = 2 * (_nbytes((bm, kdim), BF16) + 2 * _nbytes((kdim, bn), F32) + _nbytes((bm, bn), BF16))
    vmem += 2 * _nbytes((kdim, bn), BF16) + 5 * _nbytes((bm, bn), F32) + 2 * _nbytes((kdim, bn), F32)
    wspec = pl.BlockSpec((None, kdim, bn), lambda j, i: (layer, 0, j))
    return pl.pallas_call(
        _swiglu_up_kernel,
        out_shape=jax.ShapeDtypeStruct((m, n), BF16),
        grid=(n // bn, m // bm),
        in_specs=[pl.BlockSpec((bm, kdim), lambda j, i: (i, 0)), wspec, wspec],
        out_specs=pl.BlockSpec((bm, bn), lambda j, i: (i, j)),
        scratch_shapes=[pltpu.VMEM((kdim, bn), BF16), pltpu.VMEM((kdim, bn), BF16)],
        compiler_params=_params(("parallel", "arbitrary"), vmem),
        name="swiglu_up",
    )(x, wg, wu)


def _memattn_kernel(x_ref, gq_ref, wq_ref, k_ref, v_ref, wo_ref, gf_ref, xo_ref, ho_ref):
    x = x_ref[...]
    h = _rms(x, gq_ref[...]).astype(BF16)
    q = jnp.dot(h, wq_ref[...], preferred_element_type=F32).astype(BF16)
    k = k_ref[0]
    v = v_ref[0]
    scale = MEM_HEAD_DIM ** -0.5
    outs = []
    for hh in range(MEM_HEADS):
        sl = slice(hh * MEM_HEAD_DIM, (hh + 1) * MEM_HEAD_DIM)
        s = lax.dot_general(q[:, sl], k[:, sl], (((1,), (1,)), ((), ())),
                            preferred_element_type=F32) * scale
        p = _softmax_rows(s).astype(BF16)
        outs.append(jnp.dot(p, v[:, sl], preferred_element_type=F32).astype(BF16))
    o = jnp.concatenate(outs, axis=1)
    x2 = x + jnp.dot(o, wo_ref[...], preferred_element_type=F32)
    xo_ref[...] = x2
    ho_ref[...] = _rms(x2, gf_ref[...]).astype(BF16)


def _mem_xattn(x, rows_per_batch, kv, g_q, wq, wo, layer, g_ffn):
    n, d = x.shape
    nb, tm, width2 = kv.shape
    width = width2 // 2
    bm = _tile(rows_per_batch, 256, V7X_SUBLANES * 2)
    per_batch = rows_per_batch // bm
    vmem = 2 * (2 * _nbytes((bm, d), F32) + _nbytes((bm, d), BF16) + 2 * _nbytes((d, width), BF16)
                + 2 * _nbytes((tm, width), BF16))
    vmem += 4 * _nbytes((bm, d), F32)
    return pl.pallas_call(
        _memattn_kernel,
        out_shape=(jax.ShapeDtypeStruct((n, d), F32), jax.ShapeDtypeStruct((n, d), BF16)),
        grid=(n // bm,),
        in_specs=[pl.BlockSpec((bm, d), lambda i: (i, 0)),
                  pl.BlockSpec((1, d), lambda i: (0, 0)),
                  pl.BlockSpec((None, d, width), lambda i: (layer, 0, 0)),
                  pl.BlockSpec((1, tm, width), lambda i: (i // per_batch, 0, 0)),
                  pl.BlockSpec((1, tm, width), lambda i: (i // per_batch, 0, 1)),
                  pl.BlockSpec((None, width, d), lambda i: (layer, 0, 0)),
                  pl.BlockSpec((1, d), lambda i: (0, 0))],
        out_specs=(pl.BlockSpec((bm, d), lambda i: (i, 0)),
                   pl.BlockSpec((bm, d), lambda i: (i, 0))),
        compiler_params=_params(("parallel",), vmem),
        name="mem_xattn",
    )(x, g_q.reshape(1, d), wq, kv, kv, wo, g_ffn.reshape(1, d))


def _strided_rows(ref, slabs, row0, r):
    return jnp.concatenate(
        [ref[s, pl.ds(row0 + r, V7X_SUBLANES, stride=V7X_SUBLANES), :] for s in slabs], axis=1)


def _store_strided_rows(ref, slabs, row0, r, val):
    for k, s in enumerate(slabs):
        ref[s, pl.ds(row0 + r, V7X_SUBLANES, stride=V7X_SUBLANES), :] = val[:, k * V7X_LANES:(k + 1) * V7X_LANES]


def _conv_row_block(xs_ref, slabs, base, cw, cb, ntaps):
    nsl = V7X_SUBLANES
    ext = {r: _strided_rows(xs_ref, slabs, base, r) for r in range(nsl)}
    row = lax.broadcasted_iota(jnp.int32, ext[0].shape, 0)
    for k in range(1, ntaps):
        above = jnp.concatenate([xs_ref[s, pl.ds(base - k, 1), :] for s in slabs], axis=1)
        ext[-k] = jnp.where(row >= 1, pltpu.roll(ext[nsl - k], 1, axis=0), above)
    out = []
    for r in range(nsl):
        y = cb
        for j in range(ntaps):
            y = y + cw[j:j + 1, :] * ext[r - (ntaps - 1) + j]
        out.append(y)
    return out


def _lru_kernel(xin_ref, gate_ref, cw_ref, cb_ref, wax_ref, bax_ref, lam_ref, o_ref,
                xs_ref, xc_ref, a_ref, u_ref, h_ref, *, tb, nblk, blk):
    t = pl.program_id(2)
    pad = CONV_PAD_ROWS
    lanes = V7X_LANES
    nsl = V7X_SUBLANES
    nslab = nblk * blk // lanes
    per_blk = blk // lanes
    nrb = tb // ROW_BLOCK

    @pl.when(t == 0)
    def _():
        xs_ref[:, 0:pad, :] = jnp.zeros((nslab, pad, lanes), F32)
        h_ref[...] = jnp.zeros(h_ref.shape, F32)

    @pl.when(t > 0)
    def _():
        xs_ref[:, 0:pad, :] = xs_ref[:, tb:tb + pad, :]

    for s in range(nslab):
        xs_ref[s, pad:pad + tb, :] = xin_ref[:, s * lanes:(s + 1) * lanes]

    for nb in range(nblk):
        slabs = list(range(nb * per_blk, (nb + 1) * per_blk))
        sl = slice(nb * blk, (nb + 1) * blk)
        cw = cw_ref[:, sl]
        cb = cb_ref[:, sl]

        def conv_body(b, carry, slabs=slabs, cw=cw, cb=cb):
            r0 = pl.multiple_of(b * ROW_BLOCK, ROW_BLOCK)
            ys = _conv_row_block(xs_ref, slabs, pad + r0, cw, cb, LRU_CONV)
            for r in range(nsl):
                _store_strided_rows(xc_ref, slabs, r0, r, ys[r])
            return carry

        lax.fori_loop(0, nrb, conv_body, 0)

    half_neg_sp = (-0.5 * LRU_C) * jax.nn.softplus(-lam_ref[...])
    for nb in range(nblk):
        slabs = list(range(nb * per_blk, (nb + 1) * per_blk))
        sl = slice(nb * blk, (nb + 1) * blk)
        xcb = jnp.concatenate([xc_ref[s] for s in slabs], axis=1)
        half_gates = jnp.dot(xcb.astype(BF16), wax_ref[nb], preferred_element_type=F32) + bax_ref[nb]
        tr = jnp.tanh(half_gates[:, :blk])
        ig = 0.5 * jnp.tanh(half_gates[:, blk:]) + 0.5
        log_a = half_neg_sp[:, sl] * tr + half_neg_sp[:, sl]
        th = jnp.tanh(log_a)
        one_minus_a2 = -2.0 * th / (1.0 - th)
        a = jnp.exp(log_a)
        u = jnp.sqrt(one_minus_a2) * (ig * xcb)
        for k, s in enumerate(slabs):
            a_ref[s] = a[:, k * lanes:(k + 1) * lanes]
            u_ref[s] = u[:, k * lanes:(k + 1) * lanes]

    for nb in range(nblk):
        slabs = list(range(nb * per_blk, (nb + 1) * per_blk))
        sl = slice(nb * blk, (nb + 1) * blk)
        row = lax.broadcasted_iota(jnp.int32, (nsl, blk), 0)

        def scan_body(b, h, slabs=slabs, row=row):
            r0 = pl.multiple_of(b * ROW_BLOCK, ROW_BLOCK)
            acc_a, acc_h = [], []
            for r in range(nsl):
                a = _strided_rows(a_ref, slabs, r0, r)
                u = _strided_rows(u_ref, slabs, r0, r)
                if r > 0:
                    u = a * acc_h[-1] + u
                    a = a * acc_a[-1]
                acc_a.append(a)
                acc_h.append(u)
            ga, gh = acc_a[-1], acc_h[-1]
            for sh in (1, 2, 4):
                ga_prev = jnp.where(row >= sh, pltpu.roll(ga, sh, axis=0), 1.0)
                gh_prev = jnp.where(row >= sh, pltpu.roll(gh, sh, axis=0), 0.0)
                gh = ga * gh_prev + gh
                ga = ga * ga_prev
            ends = gh + ga * h
            enter = jnp.where(row >= 1, pltpu.roll(ends, 1, axis=0), h)
            for r in range(nsl):
                _store_strided_rows(u_ref, slabs, r0, r, acc_h[r] + acc_a[r] * enter)
            return ends[nsl - 1:nsl, :]

        h_ref[:, sl] = lax.fori_loop(0, nrb, scan_body, h_ref[:, sl], unroll=2)

    hs = jnp.concatenate([u_ref[s] for s in range(nslab)], axis=1)
    o_ref[...] = (hs * gate_ref[...].astype(F32)).astype(o_ref.dtype)


def _lru_core(xin, gate, rows_per_batch, conv_w, conv_b, w_ax, b_ax, lam):
    n, r = xin.shape
    nblocks, blk = w_ax.shape[0], w_ax.shape[1]
    nblk = _tile(nblocks, 3, 1)
    c = nblk * blk
    tb = _tile(rows_per_batch, 512, ROW_BLOCK)
    nt = rows_per_batch // tb
    nbat = n // rows_per_batch
    pad = CONV_PAD_ROWS
    nslab = c // V7X_LANES
    assert blk % V7X_LANES == 0 and tb % ROW_BLOCK == 0
    vmem = 2 * (_nbytes((tb, c), F32) + 2 * _nbytes((tb, c), BF16) + _nbytes((nblk, blk, 2 * blk), BF16))
    vmem += _nbytes((tb + pad, c), F32) + 3 * _nbytes((tb, c), F32) + 6 * _nbytes((tb, c), F32)
    return pl.pallas_call(
        functools.partial(_lru_kernel, tb=tb, nblk=nblk, blk=blk),
        out_shape=jax.ShapeDtypeStruct((n, r), BF16),
        grid=(nbat, r // c, nt),
        in_specs=[pl.BlockSpec((tb, c), lambda b, j, t: (b * nt + t, j)),
                  pl.BlockSpec((tb, c), lambda b, j, t: (b * nt + t, j)),
                  pl.BlockSpec((LRU_CONV, c), lambda b, j, t: (0, j)),
                  pl.BlockSpec((1, c), lambda b, j, t: (0, j)),
                  pl.BlockSpec((nblk, blk, 2 * blk), lambda b, j, t: (j, 0, 0)),
                  pl.BlockSpec((nblk, 1, 2 * blk), lambda b, j, t: (j, 0, 0)),
                  pl.BlockSpec((1, c), lambda b, j, t: (0, j))],
        out_specs=pl.BlockSpec((tb, c), lambda b, j, t: (b * nt + t, j)),
        scratch_shapes=[pltpu.VMEM((nslab, tb + pad, V7X_LANES), F32),
                        pltpu.VMEM((nslab, tb, V7X_LANES), F32),
                        pltpu.VMEM((nslab, tb, V7X_LANES), F32),
                        pltpu.VMEM((nslab, tb, V7X_LANES), F32),
                        pltpu.VMEM((1, c), F32)],
        compiler_params=_params(("parallel", "parallel", "arbitrary"), vmem),
        name="lru_core",
    )(xin, gate, conv_w, conv_b.reshape(1, r), w_ax, b_ax, lam.reshape(1, r))


def _lru_layer(x, h, rows_per_batch, layer, w_in, b_in, w_gate, b_gate, conv_w, conv_b,
               w_a, b_a, w_x, b_x, lam, w_out, b_out):
    nblocks, blk = w_a.shape[0], w_a.shape[1]
    gate = _matmul(h, w_gate, layer=layer, bias=b_gate, act="gelu", out_dtype=BF16,
                   bn=768, name="lru_gate")
    xin = _matmul(h, w_in, layer=layer, bias=b_in, out_dtype=F32, bn=768, name="lru_in")
    w_ax = (0.5 * jnp.concatenate([w_a, w_x], axis=-1)).astype(BF16)
    b_ax = 0.5 * jnp.concatenate([b_a.reshape(nblocks, 1, blk), b_x.reshape(nblocks, 1, blk)], axis=-1)
    hs = _lru_core(xin, gate, rows_per_batch, conv_w, conv_b, w_ax, b_ax, lam)
    return _matmul(hs, w_out, layer=layer, bias=b_out, residual=x, out_dtype=F32,
                   bm=512, name="lru_out")


def _split3(v):
    hi = v.astype(BF16)
    r1 = v - hi.astype(F32)
    mid = r1.astype(BF16)
    lo = (r1 - mid.astype(F32)).astype(BF16)
    return hi, mid, lo


def _ssd_kernel(z_ref, xbc_ref, dtr_ref, cw_ref, cb_ref, dtb_ref, alog_ref, dsk_ref, nw_ref, y_ref,
                xs_ref, xa_ref, st_ref, ct2_ref, dt2_ref, wt2_ref, *, d_inner):
    c = pl.program_id(1)
    L = CHUNK
    pad = CONV_PAD_ROWS
    lanes = V7X_LANES
    half = lanes // 2
    gw = d_inner // SSD_GROUPS
    pairs_per_group = gw // lanes
    npair = d_inner // lanes
    assert SSD_HEAD_DIM == half and SSD_STATE == lanes and 2 * npair == lanes and L == half

    nslab = xs_ref.shape[0]
    slabs_per_pass = SSD_CONV_LANES // lanes
    assert L == ROW_BLOCK

    @pl.when(c == 0)
    def _():
        xs_ref[:, 0:pad, :] = jnp.zeros((nslab, pad, lanes), F32)
        st_ref[...] = jnp.zeros(st_ref.shape, F32)

    @pl.when(c > 0)
    def _():
        xs_ref[:, 0:pad, :] = xs_ref[:, L:L + pad, :]

    def conv_body(i, carry):
        l0 = pl.multiple_of(i * SSD_CONV_LANES, SSD_CONV_LANES)
        slabs = [i * slabs_per_pass + k for k in range(slabs_per_pass)]
        for k, s in enumerate(slabs):
            xs_ref[s, pad:pad + L, :] = xbc_ref[:, pl.ds(l0 + k * lanes, lanes)]
        ys = _conv_row_block(xs_ref, slabs, pad, cw_ref[:, pl.ds(l0, SSD_CONV_LANES)],
                             cb_ref[:, pl.ds(l0, SSD_CONV_LANES)], SSD_CONV)
        for r in range(V7X_SUBLANES):
            _store_strided_rows(xa_ref, slabs, 0, r, jax.nn.silu(ys[r]))
        return carry

    lax.fori_loop(0, nslab // slabs_per_pass, conv_body, 0)

    dt = jax.nn.softplus(dtr_ref[...] + dtb_ref[...])
    da = dt * (-jnp.exp(alog_ref[...]))
    ri = lax.broadcasted_iota(jnp.int32, (L, L), 0)
    ci = lax.broadcasted_iota(jnp.int32, (L, L), 1)
    tril = jnp.where(ri >= ci, 1.0, 0.0).astype(BF16)
    parts = jnp.dot(tril, jnp.concatenate(_split3(da), axis=1), preferred_element_type=F32)
    cum = (parts[:, :lanes] + parts[:, lanes:2 * lanes]) + parts[:, 2 * lanes:]
    last = cum[L - 1:L, :]
    w_end = dt * jnp.exp(last - cum)
    dec_chunk = jnp.exp(last)

    def pair_rows(v):
        vt = v.T
        return jnp.concatenate([vt[:npair], vt[npair:]], axis=1)

    ct2_ref[...] = pair_rows(cum)
    dt2_ref[...] = pair_rows(dt)
    wt2_ref[...] = pair_rows(w_end)

    lane_i = lax.broadcasted_iota(jnp.int32, (L, lanes), 1)
    row_i = lax.broadcasted_iota(jnp.int32, (L, lanes), 0)
    lo_half = lane_i < half
    causal2 = row_i >= jnp.where(lo_half, lane_i, lane_i - half)
    lo_half_row = lax.broadcasted_iota(jnp.int32, (1, lanes), 1) < half
    lo_half_sq = lax.broadcasted_iota(jnp.int32, (lanes, lanes), 1) < half
    top_half_sq = lax.broadcasted_iota(jnp.int32, (lanes, lanes), 0) < half
    blockdiag = lo_half_sq == top_half_sq

    def group_body(g, carry):
        x0 = pl.multiple_of(g * gw, gw)
        p0 = pl.multiple_of(g * pairs_per_group, pairs_per_group)
        bg = xa_ref[npair + g]
        cg = xa_ref[npair + SSD_GROUPS + g].astype(BF16)
        b2 = jnp.concatenate([bg, bg], axis=0)
        cb2 = lax.dot_general(cg, b2.astype(BF16), (((1,), (1,)), ((), ())),
                              preferred_element_type=F32)
        bt2 = b2.T
        sg = st_ref[g]
        yoff = jnp.dot(cg, sg.astype(BF16), preferred_element_type=F32)
        shift = lax.rem(lanes - g * pairs_per_group, lanes)
        cum_r = pltpu.roll(cum, shift, axis=1)
        dec_r = pltpu.roll(dec_chunk, shift, axis=1)
        ct8 = ct2_ref[pl.ds(p0, pairs_per_group), :]
        dt8 = dt2_ref[pl.ds(p0, pairs_per_group), :]
        wt8 = wt2_ref[pl.ds(p0, pairs_per_group), :]
        xps = [xa_ref[p0 + pj] for pj in range(pairs_per_group)]
        xg = jnp.concatenate(xps, axis=1)
        ys, snew = [], []
        for pj in range(pairs_per_group):
            ls = slice(pj * lanes, (pj + 1) * lanes)
            xp = xps[pj]
            colp = jnp.where(lo_half,
                             jnp.broadcast_to(cum_r[:, pj:pj + 1], (L, lanes)),
                             jnp.broadcast_to(cum_r[:, half + pj:half + pj + 1], (L, lanes)))
            seg = jnp.where(causal2, colp - ct8[pj:pj + 1, :], -jnp.inf)
            m_diag = (cb2 * jnp.exp(seg) * dt8[pj:pj + 1, :]).astype(BF16)
            m_state = (bt2 * wt8[pj:pj + 1, :]).astype(BF16)
            x2 = jnp.concatenate([xp, xp], axis=0)
            xbd = jnp.where(blockdiag, x2, 0.0).astype(BF16)
            res = jnp.dot(jnp.concatenate([m_diag, m_state], axis=0), xbd,
                          preferred_element_type=F32)
            ys.append(res[:L] + jnp.exp(colp) * yoff[:, ls])
            decp = jnp.where(lo_half_row,
                             jnp.broadcast_to(dec_r[:, pj:pj + 1], (1, lanes)),
                             jnp.broadcast_to(dec_r[:, half + pj:half + pj + 1], (1, lanes)))
            snew.append(decp * sg[:, ls] + res[L:])
        st_ref[g] = jnp.concatenate(snew, axis=1)
        y = jnp.concatenate(ys, axis=1) + dsk_ref[:, pl.ds(x0, gw)] * xg
        y = y * jax.nn.silu(z_ref[:, pl.ds(x0, gw)].astype(F32))
        y = _rms(y, nw_ref[:, pl.ds(x0, gw)])
        y_ref[:, pl.ds(x0, gw)] = y.astype(y_ref.dtype)
        return carry

    lax.fori_loop(0, SSD_GROUPS, group_body, 0, unroll=2)


def _ssd_core(z, xbc, dtr, rows_per_batch, conv_w, conv_b, dt_bias, a_log, d_rep, norm_w):
    n, d_inner = z.shape
    cdim = xbc.shape[1]
    nheads = dtr.shape[1]
    nc = rows_per_batch // CHUNK
    nbat = n // rows_per_batch
    gw = d_inner // SSD_GROUPS
    npair = d_inner // V7X_LANES
    pad = CONV_PAD_ROWS
    assert cdim % SSD_CONV_LANES == 0
    vmem = 2 * (2 * _nbytes((CHUNK, d_inner), BF16) + _nbytes((CHUNK, cdim), F32)
                + _nbytes((SSD_CONV + 1, cdim), F32))
    vmem += _nbytes((CHUNK + pad, cdim), F32) + _nbytes((CHUNK, cdim), F32)
    vmem += _nbytes((SSD_GROUPS, SSD_STATE, gw), F32) + 8 * _nbytes((SSD_STATE, gw), F32)
    row = lambda b, c: (b * nc + c, 0)
    const = lambda b, c: (0, 0)
    return pl.pallas_call(
        functools.partial(_ssd_kernel, d_inner=d_inner),
        out_shape=jax.ShapeDtypeStruct((n, d_inner), BF16),
        grid=(nbat, nc),
        in_specs=[pl.BlockSpec((CHUNK, d_inner), row),
                  pl.BlockSpec((CHUNK, cdim), row),
                  pl.BlockSpec((CHUNK, nheads), row),
                  pl.BlockSpec((SSD_CONV, cdim), const),
                  pl.BlockSpec((1, cdim), const),
                  pl.BlockSpec((1, nheads), const),
                  pl.BlockSpec((1, nheads), const),
                  pl.BlockSpec((1, d_inner), const),
                  pl.BlockSpec((1, d_inner), const)],
        out_specs=pl.BlockSpec((CHUNK, d_inner), row),
        scratch_shapes=[pltpu.VMEM((cdim // V7X_LANES, CHUNK + pad, V7X_LANES), F32),
                        pltpu.VMEM((cdim // V7X_LANES, CHUNK, V7X_LANES), F32),
                        pltpu.VMEM((SSD_GROUPS, SSD_STATE, gw), F32),
                        pltpu.VMEM((npair, V7X_LANES), F32),
                        pltpu.VMEM((npair, V7X_LANES), F32),
                        pltpu.VMEM((npair, V7X_LANES), F32)],
        compiler_params=_params(("parallel", "arbitrary"), vmem),
        name="ssd_core",
    )(z, xbc, dtr, conv_w, conv_b.reshape(1, cdim), dt_bias.reshape(1, nheads),
      a_log.reshape(1, nheads), d_rep.reshape(1, d_inner), norm_w.reshape(1, d_inner))


def _ssd_layer(x, h, rows_per_batch, layer, w_in, w_in_f32, conv_w, conv_b, dt_bias, a_log, d_skip,
               norm_w, w_out):
    nheads = dt_bias.shape[0]
    d_inner = nheads * SSD_HEAD_DIM
    cdim = conv_w.shape[1]
    perm = np.concatenate([np.arange(0, nheads, 2), np.arange(1, nheads, 2)])
    w_dt = w_in_f32[layer, :, d_inner + cdim:][:, perm].astype(BF16)[None]
    z = _matmul(h, w_in, layer=layer, ncols=d_inner, out_dtype=BF16, name="ssd_in_z")
    xbc = _matmul(h, w_in, layer=layer, col0=d_inner, ncols=cdim, out_dtype=F32, name="ssd_in_xbc")
    dtr = _matmul(h, w_dt, out_dtype=F32, name="ssd_in_dt")
    d_rep = jnp.repeat(d_skip.astype(F32), SSD_HEAD_DIM)
    y = _ssd_core(z, xbc, dtr, rows_per_batch, conv_w, conv_b, dt_bias[perm], a_log[perm], d_rep, norm_w)
    return _matmul(y, w_out, layer=layer, residual=x, out_dtype=F32, bm=512, bn=512, name="ssd_out")


def _bias_gather_kernel(idx_ref, tab_ref, o_ref, *, heads):
    idx = idx_ref[...]
    ntab = tab_ref.shape[1]
    rows = lax.broadcasted_iota(jnp.int32, (ntab, idx.shape[1]), 0)
    onehot = jnp.where(rows == idx, 1.0, 0.0).astype(BF16)
    r = jnp.dot(tab_ref[...], onehot, preferred_element_type=F32)
    o_ref[...] = (r[0:heads] + r[heads:2 * heads]) + r[2 * heads:3 * heads]


def _rel_bias(rel_table):
    heads, ntab = rel_table.shape
    pad = LOOKBACK_CHUNKS * CHUNK
    band = pad + CHUNK
    rel = (np.arange(CHUNK)[:, None] + pad) - np.arange(band)[None, :]
    idx = (np.clip(rel, -MAX_REL_DIST, MAX_REL_DIST) + MAX_REL_DIST).astype(np.int32).reshape(1, -1)
    npos = idx.shape[1]
    ntab_p = -(-ntab // V7X_LANES) * V7X_LANES
    tab = jnp.pad(rel_table.astype(F32), ((0, 0), (0, ntab_p - ntab)))
    tab3 = jnp.concatenate(_split3(tab), axis=0)
    tp = _tile(npos, 2048, V7X_LANES)
    out = pl.pallas_call(
        functools.partial(_bias_gather_kernel, heads=heads),
        out_shape=jax.ShapeDtypeStruct((heads, npos), F32),
        grid=(npos // tp,),
        in_specs=[pl.BlockSpec((1, tp), lambda i: (0, i)),
                  pl.BlockSpec((3 * heads, ntab_p), lambda i: (0, 0))],
        out_specs=pl.BlockSpec((heads, tp), lambda i: (0, i)),
        compiler_params=_params(("parallel",), 16 * 2**20),
        name="rel_bias_gather",
    )(jnp.asarray(idx), tab3)
    bias = out.reshape(heads, CHUNK, band)
    masked = jnp.full((heads, CHUNK, CHUNK), -jnp.inf, F32)
    return jnp.stack([jnp.concatenate([bias, masked], axis=2),
                      jnp.concatenate([masked, bias], axis=2)], axis=1)


def _attn_kernel(q_ref, kp_ref, kc_ref, vp_ref, vc_ref, bias_ref, o_ref, kb_ref, vb_ref, *, tq, hg):
    t = pl.program_id(1)
    pad = LOOKBACK_CHUNKS * CHUNK
    hd = ATTN_HEAD_DIM
    qb = ATTN_QCHUNKS * CHUNK
    kbw = pad + qb
    assert tq == pad and ATTN_QCHUNKS % 2 == 0 and ATTN_WINDOW == pad + 2 * CHUNK
    kb_ref[0:pad, :] = kp_ref[...]
    kb_ref[pad:pad + tq, :] = kc_ref[...]
    vb_ref[0:pad, :] = vp_ref[...]
    vb_ref[pad:pad + tq, :] = vc_ref[...]
    col = lax.broadcasted_iota(jnp.int32, (CHUNK, ATTN_WINDOW), 1)

    def block_body(sb, carry):
        r0 = pl.multiple_of(sb * qb, qb)
        first_key = t * tq + sb * qb - pad
        for h in range(hg):
            ls = slice(h * hd, (h + 1) * hd)
            qh = q_ref[pl.ds(r0, qb), ls]
            kb = kb_ref[pl.ds(r0, kbw), ls]
            vb = vb_ref[pl.ds(r0, kbw), ls]
            s = lax.dot_general(qh, kb, (((1,), (1,)), ((), ())), preferred_element_type=F32)
            prow = []
            for c in range(ATTN_QCHUNKS):
                w0 = (c // 2) * 2 * CHUNK
                sc = s[c * CHUNK:(c + 1) * CHUNK, w0:w0 + ATTN_WINDOW] + bias_ref[h, c % 2]
                sc = jnp.where(first_key + w0 + col >= 0, sc, -jnp.inf)
                p = _softmax_rows(sc).astype(BF16)
                parts = [p]
                if w0 > 0:
                    parts.insert(0, jnp.zeros((CHUNK, w0), BF16))
                if w0 + ATTN_WINDOW < kbw:
                    parts.append(jnp.zeros((CHUNK, kbw - w0 - ATTN_WINDOW), BF16))
                prow.append(jnp.concatenate(parts, axis=1))
            o = jnp.dot(jnp.concatenate(prow, axis=0), vb, preferred_element_type=F32)
            o_ref[pl.ds(r0, qb), ls] = o.astype(o_ref.dtype)
        return carry

    lax.fori_loop(0, tq // qb, block_body, 0, unroll=True)


def _attn_core(q, k, v, bias, rows_per_batch):
    n, d = q.shape
    heads = d // ATTN_HEAD_DIM
    tq = LOOKBACK_CHUNKS * CHUNK
    nt = rows_per_batch // tq
    nbat = n // rows_per_batch
    hg = _tile(heads, 8, 1)
    gwid = hg * ATTN_HEAD_DIM
    kbw = tq + ATTN_QCHUNKS * CHUNK
    cur = lambda b, t, g: (b * nt + t, g)
    prev = lambda b, t, g: (b * nt + jnp.maximum(t - 1, 0), g)
    vmem = 2 * (6 * _nbytes((tq, gwid), BF16) + _nbytes((hg, 2, CHUNK, ATTN_WINDOW), F32))
    vmem += 2 * _nbytes((2 * tq, gwid), BF16) + 4 * hg * _nbytes((ATTN_QCHUNKS * CHUNK, kbw), F32)
    return pl.pallas_call(
        functools.partial(_attn_kernel, tq=tq, hg=hg),
        out_shape=jax.ShapeDtypeStruct((n, d), BF16),
        grid=(nbat, nt, heads // hg),
        in_specs=[pl.BlockSpec((tq, gwid), cur),
                  pl.BlockSpec((tq, gwid), prev),
                  pl.BlockSpec((tq, gwid), cur),
                  pl.BlockSpec((tq, gwid), prev),
                  pl.BlockSpec((tq, gwid), cur),
                  pl.BlockSpec((hg, 2, CHUNK, ATTN_WINDOW), lambda b, t, g: (g, 0, 0, 0))],
        out_specs=pl.BlockSpec((tq, gwid), cur),
        scratch_shapes=[pltpu.VMEM((2 * tq, gwid), BF16),
                        pltpu.VMEM((2 * tq, gwid), BF16)],
        compiler_params=_params(("parallel", "parallel", "parallel"), vmem),
        name="band_attn",
    )(q, k, k, v, v, bias)


def _attn_layer(x, h, rows_per_batch, layer, wq, wk, wv, wo, rel_table):
    q = _matmul(h, wq, layer=layer, scale=ATTN_HEAD_DIM ** -0.5, out_dtype=BF16, name="attn_q")
    k = _matmul(h, wk, layer=layer, out_dtype=BF16, name="attn_k")
    v = _matmul(h, wv, layer=layer, out_dtype=BF16, name="attn_v")
    o = _attn_core(q, k, v, _rel_bias(rel_table), rows_per_batch)
    return _matmul(o, wo, layer=layer, residual=x, out_dtype=F32, name="attn_out")


def _ffn(x, h, layer, w_gate, w_up, w_down):
    act = _swiglu_up(h, w_gate, w_up, layer)
    return _matmul(act, w_down, layer=layer, residual=x, out_dtype=F32, bm=512, bn=512, name="ffn_down")


def kernel(x, mem, norm_mix, norm_mem_q, norm_mem_kv, norm_ffn, norm_final, mem_wq, mem_wk, mem_wv, mem_wo, ffn_w_gate, ffn_w_up, ffn_w_down, lru_w_in, lru_b_in, lru_w_gate, lru_b_gate, lru_conv_w, lru_conv_b, lru_w_a, lru_b_a, lru_w_x, lru_b_x, lru_lambda, lru_w_out, lru_b_out, ssd_w_in, ssd_conv_w, ssd_conv_b, ssd_dt_bias, ssd_a_log, ssd_d, ssd_norm, ssd_w_out, attn_wq, attn_wk, attn_wv, attn_wo, attn_rel_bias):
    nbat, seq, d = x.shape
    tm = mem.shape[1]
    depth = norm_mix.shape[0]
    xf = x.reshape(nbat * seq, d)
    memf = mem.reshape(nbat * tm, d)
    bf = lambda w: w.astype(BF16)
    lru_wi, lru_wg, lru_wo = bf(lru_w_in), bf(lru_w_gate), bf(lru_w_out)
    ssd_wi, ssd_wo = bf(ssd_w_in), bf(ssd_w_out)
    a_wq, a_wk, a_wv, a_wo = bf(attn_wq), bf(attn_wk), bf(attn_wv), bf(attn_wo)
    m_wq, m_wo = bf(mem_wq), bf(mem_wo)
    m_wkv = bf(jnp.concatenate([mem_wk, mem_wv], axis=2))
    ffn_wd = bf(ffn_w_down)
    for i in range(depth):
        kind, j = i % N_MIXERS, i // N_MIXERS
        h = _rmsnorm(xf, norm_mix[i], BF16)
        if kind == 0:
            xf = _lru_layer(xf, h, seq, j, lru_wi, lru_b_in[j], lru_wg, lru_b_gate[j],
                            lru_conv_w[j], lru_conv_b[j], lru_w_a[j], lru_b_a[j], lru_w_x[j], lru_b_x[j],
                            lru_lambda[j], lru_wo, lru_b_out[j])
        elif kind == 1:
            xf = _ssd_layer(xf, h, seq, j, ssd_wi, ssd_w_in, ssd_conv_w[j], ssd_conv_b[j], ssd_dt_bias[j],
                            ssd_a_log[j], ssd_d[j], ssd_norm[j], ssd_wo)
        else:
            xf = _attn_layer(xf, h, seq, j, a_wq, a_wk, a_wv, a_wo, attn_rel_bias[j])
        hm = _rmsnorm(memf, norm_mem_kv[i], BF16)
        kv = _matmul(hm, m_wkv, layer=i, out_dtype=BF16, name="mem_kv").reshape(nbat, tm, -1)
        xf, h2 = _mem_xattn(xf, seq, kv, norm_mem_q[i], m_wq, m_wo, i, norm_ffn[i])
        xf = _ffn(xf, h2, i, ffn_w_gate, ffn_w_up, ffn_wd)
    return _rmsnorm(xf, norm_final, F32).reshape(nbat, seq, d)
```

```python
import functools

import numpy as np
import jax
import jax.numpy as jnp
from jax import lax
from jax.experimental import pallas as pl
from jax.experimental.pallas import tpu as pltpu

F32 = jnp.float32
BF16 = jnp.bfloat16

V7X_LANES = 128
V7X_SUBLANES = 8
V7X_VMEM_BYTES = 64 * 2**20
V7X_VMEM_LIMIT_CAP = V7X_VMEM_BYTES - 6 * 2**20
COMPILER_SCRATCH_ALLOWANCE = 4 * 2**20

RMS_EPS = 1e-6
N_MIXERS = 3
CHUNK = 64
LRU_C = 8.0
LRU_CONV = 4
SSD_HEAD_DIM = 64
SSD_STATE = 128
SSD_GROUPS = 8
SSD_CONV = 4
SSD_CONV_LANES = 4 * V7X_LANES
ATTN_HEAD_DIM = 128
LOOKBACK_CHUNKS = 8
MAX_REL_DIST = 256
ATTN_QCHUNKS = 4
ATTN_WINDOW = (LOOKBACK_CHUNKS + 2) * CHUNK
MEM_HEADS = 4
MEM_HEAD_DIM = 128
CONV_PAD_ROWS = V7X_SUBLANES
ROW_BLOCK = V7X_SUBLANES * V7X_SUBLANES


def _tile(dim, target, align):
    if dim <= target:
        return dim
    best = None
    for t in range(align, target + 1, align):
        if dim % t == 0:
            best = t
    assert best is not None, (dim, target, align)
    return best


def _params(semantics, vmem_bytes):
    limit = vmem_bytes + COMPILER_SCRATCH_ALLOWANCE
    return pltpu.CompilerParams(
        dimension_semantics=semantics,
        vmem_limit_bytes=int(min(max(limit, 16 * 2**20), V7X_VMEM_LIMIT_CAP)),
    )


def _nbytes(shape, dtype):
    return int(np.prod(shape)) * jnp.dtype(dtype).itemsize


def _rms(x, g):
    ms = jnp.mean(x * x, axis=-1, keepdims=True)
    return x * lax.rsqrt(ms + RMS_EPS) * g


def _softmax_rows(s):
    m = jnp.max(s, axis=-1, keepdims=True)
    e = jnp.exp(s - m)
    return e / jnp.sum(e, axis=-1, keepdims=True)


def _rmsnorm_kernel(x_ref, g_ref, o_ref):
    o_ref[...] = _rms(x_ref[...], g_ref[...]).astype(o_ref.dtype)


def _rmsnorm(x, g, out_dtype):
    n, d = x.shape
    bm = _tile(n, 256, V7X_SUBLANES)
    vmem = 2 * (_nbytes((bm, d), F32) + _nbytes((bm, d), out_dtype)) + 2 * _nbytes((bm, d), F32)
    return pl.pallas_call(
        _rmsnorm_kernel,
        out_shape=jax.ShapeDtypeStruct((n, d), out_dtype),
        grid=(n // bm,),
        in_specs=[pl.BlockSpec((bm, d), lambda i: (i, 0)),
                  pl.BlockSpec((1, d), lambda i: (0, 0))],
        out_specs=pl.BlockSpec((bm, d), lambda i: (i, 0)),
        compiler_params=_params(("parallel",), vmem),
        name="rmsnorm",
    )(x, g.reshape(1, d))


def _mm_kernel(x_ref, w_ref, *rest, has_bias, has_res, act, scale):
    rest = list(rest)
    b_ref = rest.pop(0) if has_bias else None
    r_ref = rest.pop(0) if has_res else None
    (o_ref,) = rest
    acc = jnp.dot(x_ref[...], w_ref[...], preferred_element_type=F32)
    if has_bias:
        acc = acc + b_ref[...]
    if scale is not None:
        acc = acc * scale
    if act == "gelu":
        acc = jax.nn.gelu(acc)
    if has_res:
        acc = acc + r_ref[...]
    o_ref[...] = acc.astype(o_ref.dtype)


def _matmul(x, w, *, layer=0, col0=0, ncols=None, bias=None, residual=None, act=None, scale=None,
            out_dtype=F32, bm=1024, bn=1024, name="matmul"):
    m, kdim = x.shape
    n = w.shape[2] - col0 if ncols is None else ncols
    bm = _tile(m, bm, V7X_SUBLANES * 2)
    bn = _tile(n, bn, V7X_LANES)
    assert col0 % bn == 0 and w.shape[1] == kdim
    jb0 = col0 // bn
    in_specs = [pl.BlockSpec((bm, kdim), lambda i, j: (i, 0)),
                pl.BlockSpec((None, kdim, bn), lambda i, j: (layer, 0, jb0 + j))]
    args = [x, w]
    vmem = 2 * (_nbytes((bm, kdim), x.dtype) + _nbytes((kdim, bn), w.dtype) + _nbytes((bm, bn), out_dtype))
    vmem += (2 + 2 * (act is not None)) * _nbytes((bm, bn), F32)
    if bias is not None:
        in_specs.append(pl.BlockSpec((1, bn), lambda i, j: (0, j)))
        args.append(bias.reshape(1, n).astype(F32))
    if residual is not None:
        in_specs.append(pl.BlockSpec((bm, bn), lambda i, j: (i, j)))
        args.append(residual)
        vmem += 2 * _nbytes((bm, bn), residual.dtype)
    return pl.pallas_call(
        functools.partial(_mm_kernel, has_bias=bias is not None,
                          has_res=residual is not None, act=act, scale=scale),
        out_shape=jax.ShapeDtypeStruct((m, n), out_dtype),
        grid=(m // bm, n // bn),
        in_specs=in_specs,
        out_specs=pl.BlockSpec((bm, bn), lambda i, j: (i, j)),
        compiler_params=_params(("parallel", "parallel"), vmem),
        name=name,
    )(*args)


def _swiglu_up_kernel(x_ref, wg_ref, wu_ref, o_ref, wgb_ref, wub_ref):
    @pl.when(pl.program_id(1) == 0)
    def _():
        wgb_ref[...] = wg_ref[...].astype(BF16)
        wub_ref[...] = wu_ref[...].astype(BF16)

    x = x_ref[...]
    g = jnp.dot(x, wgb_ref[...], preferred_element_type=F32)
    u = jnp.dot(x, wub_ref[...], preferred_element_type=F32)
    o_ref[...] = (jax.nn.silu(g) * u).astype(o_ref.dtype)


def _swiglu_up(x, wg, wu, layer, *, bm=1024, bn=256):
    m, kdim = x.shape
    n = wg.shape[2]
    bm = _tile(m, bm, V7X_SUBLANES * 2)
    bn = _tile(n, bn, V7X_LANES)
    vmem = 2 * (_nbytes((bm, kdim), BF16) + 2 * _nbytes((kdim, bn), F32) + _nbytes((bm, bn), BF16))
    vmem += 2 * _nbytes((kdim, bn), BF16) + 5 * _nbytes((bm, bn), F32) + 2 * _nbytes((kdim, bn), F32)
    wspec = pl.BlockSpec((None, kdim, bn), lambda j, i: (layer, 0, j))
    return pl.pallas_call(
        _swiglu_up_kernel,
        out_shape=jax.ShapeDtypeStruct((m, n), BF16),
        grid=(n // bn, m // bm),
        in_specs=[pl.BlockSpec((bm, kdim), lambda j, i: (i, 0)), wspec, wspec],
        out_specs=pl.BlockSpec((bm, bn), lambda j, i: (i, j)),
        scratch_shapes=[pltpu.VMEM((kdim, bn), BF16), pltpu.VMEM((kdim, bn), BF16)],
        compiler_params=_params(("parallel", "arbitrary"), vmem),
        name="swiglu_up",
    )(x, wg, wu)


def _memattn_kernel(x_ref, gq_ref, wq_ref, k_ref, v_ref, wo_ref, gf_ref, xo_ref, ho_ref):
    x = x_ref[...]
    h = _rms(x, gq_ref[...]).astype(BF16)
    q = jnp.dot(h, wq_ref[...], preferred_element_type=F32).astype(BF16)
    k = k_ref[0]
    v = v_ref[0]
    scale = MEM_HEAD_DIM ** -0.5
    outs = []
    for hh in range(MEM_HEADS):
        sl = slice(hh * MEM_HEAD_DIM, (hh + 1) * MEM_HEAD_DIM)
        s = lax.dot_general(q[:, sl], k[:, sl], (((1,), (1,)), ((), ())),
                            preferred_element_type=F32) * scale
        p = _softmax_rows(s).astype(BF16)
        outs.append(jnp.dot(p, v[:, sl], preferred_element_type=F32).astype(BF16))
    o = jnp.concatenate(outs, axis=1)
    x2 = x + jnp.dot(o, wo_ref[...], preferred_element_type=F32)
    xo_ref[...] = x2
    ho_ref[...] = _rms(x2, gf_ref[...]).astype(BF16)


def _mem_xattn(x, rows_per_batch, kv, g_q, wq, wo, layer, g_ffn):
    n, d = x.shape
    nb, tm, width2 = kv.shape
    width = width2 // 2
    bm = _tile(rows_per_batch, 256, V7X_SUBLANES * 2)
    per_batch = rows_per_batch // bm
    vmem = 2 * (2 * _nbytes((bm, d), F32) + _nbytes((bm, d), BF16) + 2 * _nbytes((d, width), BF16)
                + 2 * _nbytes((tm, width), BF16))
    vmem += 4 * _nbytes((bm, d), F32)
    return pl.pallas_call(
        _memattn_kernel,
        out_shape=(jax.ShapeDtypeStruct((n, d), F32), jax.ShapeDtypeStruct((n, d), BF16)),
        grid=(n // bm,),
        in_specs=[pl.BlockSpec((bm, d), lambda i: (i, 0)),
                  pl.BlockSpec((1, d), lambda i: (0, 0)),
                  pl.BlockSpec((None, d, width), lambda i: (layer, 0, 0)),
                  pl.BlockSpec((1, tm, width), lambda i: (i // per_batch, 0, 0)),
                  pl.BlockSpec((1, tm, width), lambda i: (i // per_batch, 0, 1)),
                  pl.BlockSpec((None, width, d), lambda i: (layer, 0, 0)),
                  pl.BlockSpec((1, d), lambda i: (0, 0))],
        out_specs=(pl.BlockSpec((bm, d), lambda i: (i, 0)),
                   pl.BlockSpec((bm, d), lambda i: (i, 0))),
        compiler_params=_params(("parallel",), vmem),
        name="mem_xattn",
    )(x, g_q.reshape(1, d), wq, kv, kv, wo, g_ffn.reshape(1, d))


def _strided_rows(ref, slabs, row0, r):
    return jnp.concatenate(
        [ref[s, pl.ds(row0 + r, V7X_SUBLANES, stride=V7X_SUBLANES), :] for s in slabs], axis=1)


def _store_strided_rows(ref, slabs, row0, r, val):
    for k, s in enumerate(slabs):
        ref[s, pl.ds(row0 + r, V7X_SUBLANES, stride=V7X_SUBLANES), :] = val[:, k * V7X_LANES:(k + 1) * V7X_LANES]


def _conv_row_block(xs_ref, slabs, base, cw, cb, ntaps):
    nsl = V7X_SUBLANES
    ext = {r: _strided_rows(xs_ref, slabs, base, r) for r in range(nsl)}
    row = lax.broadcasted_iota(jnp.int32, ext[0].shape, 0)
    for k in range(1, ntaps):
        above = jnp.concatenate([xs_ref[s, pl.ds(base - k, 1), :] for s in slabs], axis=1)
        ext[-k] = jnp.where(row >= 1, pltpu.roll(ext[nsl - k], 1, axis=0), above)
    out = []
    for r in range(nsl):
        y = cb
        for j in range(ntaps):
            y = y + cw[j:j + 1, :] * ext[r - (ntaps - 1) + j]
        out.append(y)
    return out


def _lru_kernel(xin_ref, gate_ref, cw_ref, cb_ref, wax_ref, bax_ref, lam_ref, o_ref,
                xs_ref, xc_ref, a_ref, u_ref, h_ref, *, tb, nblk, blk):
    t = pl.program_id(2)
    pad = CONV_PAD_ROWS
    lanes = V7X_LANES
    nsl = V7X_SUBLANES
    nslab = nblk * blk // lanes
    per_blk = blk // lanes
    nrb = tb // ROW_BLOCK

    @pl.when(t == 0)
    def _():
        xs_ref[:, 0:pad, :] = jnp.zeros((nslab, pad, lanes), F32)
        h_ref[...] = jnp.zeros(h_ref.shape, F32)

    @pl.when(t > 0)
    def _():
        xs_ref[:, 0:pad, :] = xs_ref[:, tb:tb + pad, :]

    for s in range(nslab):
        xs_ref[s, pad:pad + tb, :] = xin_ref[:, s * lanes:(s + 1) * lanes]

    for nb in range(nblk):
        slabs = list(range(nb * per_blk, (nb + 1) * per_blk))
        sl = slice(nb * blk, (nb + 1) * blk)
        cw = cw_ref[:, sl]
        cb = cb_ref[:, sl]

        def conv_body(b, carry, slabs=slabs, cw=cw, cb=cb):
            r0 = pl.multiple_of(b * ROW_BLOCK, ROW_BLOCK)
            ys = _conv_row_block(xs_ref, slabs, pad + r0, cw, cb, LRU_CONV)
            for r in range(nsl):
                _store_strided_rows(xc_ref, slabs, r0, r, ys[r])
            return carry

        lax.fori_loop(0, nrb, conv_body, 0, unroll=2)

    half_neg_sp = (-0.5 * LRU_C) * jax.nn.softplus(-lam_ref[...])
    for nb in range(nblk):
        slabs = list(range(nb * per_blk, (nb + 1) * per_blk))
        sl = slice(nb * blk, (nb + 1) * blk)
        xcb = jnp.concatenate([xc_ref[s] for s in slabs], axis=1)
        half_gates = jnp.dot(xcb.astype(BF16), wax_ref[nb], preferred_element_type=F32) + bax_ref[nb]
        tr = jnp.tanh(half_gates[:, :blk])
        ig = 0.5 * jnp.tanh(half_gates[:, blk:]) + 0.5
        log_a = half_neg_sp[:, sl] * tr + half_neg_sp[:, sl]
        th = jnp.tanh(log_a)
        one_minus_a2 = -2.0 * th / (1.0 - th)
        a = jnp.exp(log_a)
        u = jnp.sqrt(one_minus_a2) * (ig * xcb)
        for k, s in enumerate(slabs):
            a_ref[s] = a[:, k * lanes:(k + 1) * lanes]
            u_ref[s] = u[:, k * lanes:(k + 1) * lanes]

    for nb in range(nblk):
        slabs = list(range(nb * per_blk, (nb + 1) * per_blk))
        sl = slice(nb * blk, (nb + 1) * blk)
        row = lax.broadcasted_iota(jnp.int32, (nsl, blk), 0)

        def scan_body(b, h, slabs=slabs, row=row):
            r0 = pl.multiple_of(b * ROW_BLOCK, ROW_BLOCK)
            acc_a, acc_h = [], []
            for r in range(nsl):
                a = _strided_rows(a_ref, slabs, r0, r)
                u = _strided_rows(u_ref, slabs, r0, r)
                if r > 0:
                    u = a * acc_h[-1] + u
                    a = a * acc_a[-1]
                acc_a.append(a)
                acc_h.append(u)
            ga, gh = acc_a[-1], acc_h[-1]
            for sh in (1, 2, 4):
                ga_prev = jnp.where(row >= sh, pltpu.roll(ga, sh, axis=0), 1.0)
                gh_prev = jnp.where(row >= sh, pltpu.roll(gh, sh, axis=0), 0.0)
                gh = ga * gh_prev + gh
                ga = ga * ga_prev
            ends = gh + ga * h
            enter = jnp.where(row >= 1, pltpu.roll(ends, 1, axis=0), h)
            for r in range(nsl):
                _store_strided_rows(u_ref, slabs, r0, r, acc_h[r] + acc_a[r] * enter)
            return ends[nsl - 1:nsl, :]

        h_ref[:, sl] = lax.fori_loop(0, nrb, scan_body, h_ref[:, sl], unroll=2)

    hs = jnp.concatenate([u_ref[s] for s in range(nslab)], axis=1)
    o_ref[...] = (hs * gate_ref[...].astype(F32)).astype(o_ref.dtype)


def _lru_core(xin, gate, rows_per_batch, conv_w, conv_b, w_ax, b_ax, lam):
    n, r = xin.shape
    nblocks, blk = w_ax.shape[0], w_ax.shape[1]
    nblk = _tile(nblocks, 3, 1)
    c = nblk * blk
    tb = _tile(rows_per_batch, 512, ROW_BLOCK)
    nt = rows_per_batch // tb
    nbat = n // rows_per_batch
    pad = CONV_PAD_ROWS
    nslab = c // V7X_LANES
    assert blk % V7X_LANES == 0 and tb % ROW_BLOCK == 0
    vmem = 2 * (_nbytes((tb, c), F32) + 2 * _nbytes((tb, c), BF16) + _nbytes((nblk, blk, 2 * blk), BF16))
    vmem += _nbytes((tb + pad, c), F32) + 3 * _nbytes((tb, c), F32) + 6 * _nbytes((tb, c), F32)
    return pl.pallas_call(
        functools.partial(_lru_kernel, tb=tb, nblk=nblk, blk=blk),
        out_shape=jax.ShapeDtypeStruct((n, r), BF16),
        grid=(nbat, r // c, nt),
        in_specs=[pl.BlockSpec((tb, c), lambda b, j, t: (b * nt + t, j)),
                  pl.BlockSpec((tb, c), lambda b, j, t: (b * nt + t, j)),
                  pl.BlockSpec((LRU_CONV, c), lambda b, j, t: (0, j)),
                  pl.BlockSpec((1, c), lambda b, j, t: (0, j)),
                  pl.BlockSpec((nblk, blk, 2 * blk), lambda b, j, t: (j, 0, 0)),
                  pl.BlockSpec((nblk, 1, 2 * blk), lambda b, j, t: (j, 0, 0)),
                  pl.BlockSpec((1, c), lambda b, j, t: (0, j))],
        out_specs=pl.BlockSpec((tb, c), lambda b, j, t: (b * nt + t, j)),
        scratch_shapes=[pltpu.VMEM((nslab, tb + pad, V7X_LANES), F32),
                        pltpu.VMEM((nslab, tb, V7X_LANES), F32),
                        pltpu.VMEM((nslab, tb, V7X_LANES), F32),
                        pltpu.VMEM((nslab, tb, V7X_LANES), F32),
                        pltpu.VMEM((1, c), F32)],
        compiler_params=_params(("parallel", "parallel", "arbitrary"), vmem),
        name="lru_core",
    )(xin, gate, conv_w, conv_b.reshape(1, r), w_ax, b_ax, lam.reshape(1, r))


def _lru_layer(x, h, rows_per_batch, layer, w_in, b_in, w_gate, b_gate, conv_w, conv_b,
               w_a, b_a, w_x, b_x, lam, w_out, b_out):
    nblocks, blk = w_a.shape[0], w_a.shape[1]
    gate = _matmul(h, w_gate, layer=layer, bias=b_gate, act="gelu", out_dtype=BF16,
                   bn=768, name="lru_gate")
    xin = _matmul(h, w_in, layer=layer, bias=b_in, out_dtype=F32, bn=768, name="lru_in")
    w_ax = (0.5 * jnp.concatenate([w_a, w_x], axis=-1)).astype(BF16)
    b_ax = 0.5 * jnp.concatenate([b_a.reshape(nblocks, 1, blk), b_x.reshape(nblocks, 1, blk)], axis=-1)
    hs = _lru_core(xin, gate, rows_per_batch, conv_w, conv_b, w_ax, b_ax, lam)
    return _matmul(hs, w_out, layer=layer, bias=b_out, residual=x, out_dtype=F32,
                   bm=512, name="lru_out")


def _split3(v):
    hi = v.astype(BF16)
    r1 = v - hi.astype(F32)
    mid = r1.astype(BF16)
    lo = (r1 - mid.astype(F32)).astype(BF16)
    return hi, mid, lo


def _ssd_kernel(z_ref, xbc_ref, dtr_ref, cw_ref, cb_ref, dtb_ref, alog_ref, dsk_ref, nw_ref, y_ref,
                xs_ref, xa_ref, st_ref, ct2_ref, dt2_ref, wt2_ref, *, d_inner):
    c = pl.program_id(1)
    L = CHUNK
    pad = CONV_PAD_ROWS
    lanes = V7X_LANES
    half = lanes // 2
    gw = d_inner // SSD_GROUPS
    pairs_per_group = gw // lanes
    npair = d_inner // lanes
    assert SSD_HEAD_DIM == half and SSD_STATE == lanes and 2 * npair == lanes and L == half

    nslab = xs_ref.shape[0]
    slabs_per_pass = SSD_CONV_LANES // lanes
    assert L == ROW_BLOCK

    @pl.when(c == 0)
    def _():
        xs_ref[:, 0:pad, :] = jnp.zeros((nslab, pad, lanes), F32)
        st_ref[...] = jnp.zeros(st_ref.shape, F32)

    @pl.when(c > 0)
    def _():
        xs_ref[:, 0:pad, :] = xs_ref[:, L:L + pad, :]

    def conv_body(i, carry):
        l0 = pl.multiple_of(i * SSD_CONV_LANES, SSD_CONV_LANES)
        slabs = [i * slabs_per_pass + k for k in range(slabs_per_pass)]
        for k, s in enumerate(slabs):
            xs_ref[s, pad:pad + L, :] = xbc_ref[:, pl.ds(l0 + k * lanes, lanes)]
        ys = _conv_row_block(xs_ref, slabs, pad, cw_ref[:, pl.ds(l0, SSD_CONV_LANES)],
                             cb_ref[:, pl.ds(l0, SSD_CONV_LANES)], SSD_CONV)
        for r in range(V7X_SUBLANES):
            _store_strided_rows(xa_ref, slabs, 0, r, jax.nn.silu(ys[r]))
        return carry

    lax.fori_loop(0, nslab // slabs_per_pass, conv_body, 0, unroll=2)

    dt = jax.nn.softplus(dtr_ref[...] + dtb_ref[...])
    da = dt * (-jnp.exp(alog_ref[...]))
    ri = lax.broadcasted_iota(jnp.int32, (L, L), 0)
    ci = lax.broadcasted_iota(jnp.int32, (L, L), 1)
    tril = jnp.where(ri >= ci, 1.0, 0.0).astype(BF16)
    parts = jnp.dot(tril, jnp.concatenate(_split3(da), axis=1), preferred_element_type=F32)
    cum = (parts[:, :lanes] + parts[:, lanes:2 * lanes]) + parts[:, 2 * lanes:]
    last = cum[L - 1:L, :]
    w_end = dt * jnp.exp(last - cum)
    dec_chunk = jnp.exp(last)

    def pair_rows(v):
        vt = v.T
        return jnp.concatenate([vt[:npair], vt[npair:]], axis=1)

    ct2_ref[...] = pair_rows(cum)
    dt2_ref[...] = pair_rows(dt)
    wt2_ref[...] = pair_rows(w_end)

    lane_i = lax.broadcasted_iota(jnp.int32, (L, lanes), 1)
    row_i = lax.broadcasted_iota(jnp.int32, (L, lanes), 0)
    lo_half = lane_i < half
    causal2 = row_i >= jnp.where(lo_half, lane_i, lane_i - half)
    lo_half_row = lax.broadcasted_iota(jnp.int32, (1, lanes), 1) < half
    lo_half_sq = lax.broadcasted_iota(jnp.int32, (lanes, lanes), 1) < half
    top_half_sq = lax.broadcasted_iota(jnp.int32, (lanes, lanes), 0) < half
    blockdiag = lo_half_sq == top_half_sq

    def group_body(g, carry):
        x0 = pl.multiple_of(g * gw, gw)
        p0 = pl.multiple_of(g * pairs_per_group, pairs_per_group)
        bg = xa_ref[npair + g]
        cg = xa_ref[npair + SSD_GROUPS + g].astype(BF16)
        b2 = jnp.concatenate([bg, bg], axis=0)
        cb2 = lax.dot_general(cg, b2.astype(BF16), (((1,), (1,)), ((), ())),
                              preferred_element_type=F32)
        bt2 = b2.T
        sg = st_ref[g]
        yoff = jnp.dot(cg, sg.astype(BF16), preferred_element_type=F32)
        shift = lax.rem(lanes - g * pairs_per_group, lanes)
        cum_r = pltpu.roll(cum, shift, axis=1)
        dec_r = pltpu.roll(dec_chunk, shift, axis=1)
        ct8 = ct2_ref[pl.ds(p0, pairs_per_group), :]
        dt8 = dt2_ref[pl.ds(p0, pairs_per_group), :]
        wt8 = wt2_ref[pl.ds(p0, pairs_per_group), :]
        xps = [xa_ref[p0 + pj] for pj in range(pairs_per_group)]
        xg = jnp.concatenate(xps, axis=1)
        ys, snew = [], []
        for pj in range(pairs_per_group):
            ls = slice(pj * lanes, (pj + 1) * lanes)
            xp = xps[pj]
            colp = jnp.where(lo_half,
                             jnp.broadcast_to(cum_r[:, pj:pj + 1], (L, lanes)),
                             jnp.broadcast_to(cum_r[:, half + pj:half + pj + 1], (L, lanes)))
            seg = jnp.where(causal2, colp - ct8[pj:pj + 1, :], -jnp.inf)
            m_diag = (cb2 * jnp.exp(seg) * dt8[pj:pj + 1, :]).astype(BF16)
            m_state = (bt2 * wt8[pj:pj + 1, :]).astype(BF16)
            x2 = jnp.concatenate([xp, xp], axis=0)
            xbd = jnp.where(blockdiag, x2, 0.0).astype(BF16)
            res = jnp.dot(jnp.concatenate([m_diag, m_state], axis=0), xbd,
                          preferred_element_type=F32)
            ys.append(res[:L] + jnp.exp(colp) * yoff[:, ls])
            decp = jnp.where(lo_half_row,
                             jnp.broadcast_to(dec_r[:, pj:pj + 1], (1, lanes)),
                             jnp.broadcast_to(dec_r[:, half + pj:half + pj + 1], (1, lanes)))
            snew.append(decp * sg[:, ls] + res[L:])
        st_ref[g] = jnp.concatenate(snew, axis=1)
        y = jnp.concatenate(ys, axis=1) + dsk_ref[:, pl.ds(x0, gw)] * xg
        y = y * jax.nn.silu(z_ref[:, pl.ds(x0, gw)].astype(F32))
        y = _rms(y, nw_ref[:, pl.ds(x0, gw)])
        y_ref[:, pl.ds(x0, gw)] = y.astype(y_ref.dtype)
        return carry

    lax.fori_loop(0, SSD_GROUPS, group_body, 0, unroll=2)


def _ssd_core(z, xbc, dtr, rows_per_batch, conv_w, conv_b, dt_bias, a_log, d_rep, norm_w):
    n, d_inner = z.shape
    cdim = xbc.shape[1]
    nheads = dtr.shape[1]
    nc = rows_per_batch // CHUNK
    nbat = n // rows_per_batch
    gw = d_inner // SSD_GROUPS
    npair = d_inner // V7X_LANES
    pad = CONV_PAD_ROWS
    assert cdim % SSD_CONV_LANES == 0
    vmem = 2 * (2 * _nbytes((CHUNK, d_inner), BF16) + _nbytes((CHUNK, cdim), F32)
                + _nbytes((SSD_CONV + 1, cdim), F32))
    vmem += _nbytes((CHUNK + pad, cdim), F32) + _nbytes((CHUNK, cdim), F32)
    vmem += _nbytes((SSD_GROUPS, SSD_STATE, gw), F32) + 8 * _nbytes((SSD_STATE, gw), F32)
    row = lambda b, c: (b * nc + c, 0)
    const = lambda b, c: (0, 0)
    return pl.pallas_call(
        functools.partial(_ssd_kernel, d_inner=d_inner),
        out_shape=jax.ShapeDtypeStruct((n, d_inner), BF16),
        grid=(nbat, nc),
        in_specs=[pl.BlockSpec((CHUNK, d_inner), row),
                  pl.BlockSpec((CHUNK, cdim), row),
                  pl.BlockSpec((CHUNK, nheads), row),
                  pl.BlockSpec((SSD_CONV, cdim), const),
                  pl.BlockSpec((1, cdim), const),
                  pl.BlockSpec((1, nheads), const),
                  pl.BlockSpec((1, nheads), const),
                  pl.BlockSpec((1, d_inner), const),
                  pl.BlockSpec((1, d_inner), const)],
        out_specs=pl.BlockSpec((CHUNK, d_inner), row),
        scratch_shapes=[pltpu.VMEM((cdim // V7X_LANES, CHUNK + pad, V7X_LANES), F32),
                        pltpu.VMEM((cdim // V7X_LANES, CHUNK, V7X_LANES), F32),
                        pltpu.VMEM((SSD_GROUPS, SSD_STATE, gw), F32),
                        pltpu.VMEM((npair, V7X_LANES), F32),
                        pltpu.VMEM((npair, V7X_LANES), F32),
                        pltpu.VMEM((npair, V7X_LANES), F32)],
        compiler_params=_params(("parallel", "arbitrary"), vmem),
        name="ssd_core",
    )(z, xbc, dtr, conv_w, conv_b.reshape(1, cdim), dt_bias.reshape(1, nheads),
      a_log.reshape(1, nheads), d_rep.reshape(1, d_inner), norm_w.reshape(1, d_inner))


def _ssd_layer(x, h, rows_per_batch, layer, w_in, w_in_f32, conv_w, conv_b, dt_bias, a_log, d_skip,
               norm_w, w_out):
    nheads = dt_bias.shape[0]
    d_inner = nheads * SSD_HEAD_DIM
    cdim = conv_w.shape[1]
    perm = np.concatenate([np.arange(0, nheads, 2), np.arange(1, nheads, 2)])
    w_dt = w_in_f32[layer, :, d_inner + cdim:][:, perm].astype(BF16)[None]
    z = _matmul(h, w_in, layer=layer, ncols=d_inner, out_dtype=BF16, name="ssd_in_z")
    xbc = _matmul(h, w_in, layer=layer, col0=d_inner, ncols=cdim, out_dtype=F32, name="ssd_in_xbc")
    dtr = _matmul(h, w_dt, out_dtype=F32, name="ssd_in_dt")
    d_rep = jnp.repeat(d_skip.astype(F32), SSD_HEAD_DIM)
    y = _ssd_core(z, xbc, dtr, rows_per_batch, conv_w, conv_b, dt_bias[perm], a_log[perm], d_rep, norm_w)
    return _matmul(y, w_out, layer=layer, residual=x, out_dtype=F32, bm=512, bn=512, name="ssd_out")


def _bias_gather_kernel(idx_ref, tab_ref, o_ref, *, heads):
    idx = idx_ref[...]
    ntab = tab_ref.shape[1]
    rows = lax.broadcasted_iota(jnp.int32, (ntab, idx.shape[1]), 0)
    onehot = jnp.where(rows == idx, 1.0, 0.0).astype(BF16)
    r = jnp.dot(tab_ref[...], onehot, preferred_element_type=F32)
    o_ref[...] = (r[0:heads] + r[heads:2 * heads]) + r[2 * heads:3 * heads]


def _rel_bias(rel_table):
    heads, ntab = rel_table.shape
    pad = LOOKBACK_CHUNKS * CHUNK
    band = pad + CHUNK
    rel = (np.arange(CHUNK)[:, None] + pad) - np.arange(band)[None, :]
    idx = (np.clip(rel, -MAX_REL_DIST, MAX_REL_DIST) + MAX_REL_DIST).astype(np.int32).reshape(1, -1)
    npos = idx.shape[1]
    ntab_p = -(-ntab // V7X_LANES) * V7X_LANES
    tab = jnp.pad(rel_table.astype(F32), ((0, 0), (0, ntab_p - ntab)))
    tab3 = jnp.concatenate(_split3(tab), axis=0)
    tp = _tile(npos, 2048, V7X_LANES)
    out = pl.pallas_call(
        functools.partial(_bias_gather_kernel, heads=heads),
        out_shape=jax.ShapeDtypeStruct((heads, npos), F32),
        grid=(npos // tp,),
        in_specs=[pl.BlockSpec((1, tp), lambda i: (0, i)),
                  pl.BlockSpec((3 * heads, ntab_p), lambda i: (0, 0))],
        out_specs=pl.BlockSpec((heads, tp), lambda i: (0, i)),
        compiler_params=_params(("parallel",), 16 * 2**20),
        name="rel_bias_gather",
    )(jnp.asarray(idx), tab3)
    bias = out.reshape(heads, CHUNK, band)
    masked = jnp.full((heads, CHUNK, CHUNK), -jnp.inf, F32)
    return jnp.stack([jnp.concatenate([bias, masked], axis=2),
                      jnp.concatenate([masked, bias], axis=2)], axis=1)


def _attn_kernel(q_ref, kp_ref, kc_ref, vp_ref, vc_ref, bias_ref, o_ref, kb_ref, vb_ref, s_ref, p_ref,
                 *, tq, hg):
    t = pl.program_id(1)
    pad = LOOKBACK_CHUNKS * CHUNK
    hd = ATTN_HEAD_DIM
    qb = ATTN_QCHUNKS * CHUNK
    kbw = pad + qb
    assert tq == pad and ATTN_QCHUNKS % 2 == 0 and ATTN_WINDOW == pad + 2 * CHUNK
    kb_ref[0:pad, :] = kp_ref[...]
    kb_ref[pad:pad + tq, :] = kc_ref[...]
    vb_ref[0:pad, :] = vp_ref[...]
    vb_ref[pad:pad + tq, :] = vc_ref[...]
    col = lax.broadcasted_iota(jnp.int32, (CHUNK, ATTN_WINDOW), 1)

    def block_body(sb, carry):
        r0 = pl.multiple_of(sb * qb, qb)
        first_key = t * tq + sb * qb - pad
        for h in range(hg):
            ls = slice(h * hd, (h + 1) * hd)
            qh = q_ref[pl.ds(r0, qb), ls]
            kb = kb_ref[pl.ds(r0, kbw), ls]
            s_ref[h] = lax.dot_general(qh, kb, (((1,), (1,)), ((), ())), preferred_element_type=F32)
        for h in range(hg):
            for c in range(ATTN_QCHUNKS):
                w0 = (c // 2) * 2 * CHUNK
                rows = slice(c * CHUNK, (c + 1) * CHUNK)
                sc = s_ref[h, rows, w0:w0 + ATTN_WINDOW] + bias_ref[h, c % 2]
                sc = jnp.where(first_key + w0 + col >= 0, sc, -jnp.inf)
                p_ref[h, rows, w0:w0 + ATTN_WINDOW] = _softmax_rows(sc).astype(BF16)
                if w0 > 0:
                    p_ref[h, rows, 0:w0] = jnp.zeros((CHUNK, w0), BF16)
                if w0 + ATTN_WINDOW < kbw:
                    p_ref[h, rows, w0 + ATTN_WINDOW:kbw] = jnp.zeros((CHUNK, kbw - w0 - ATTN_WINDOW), BF16)
        for h in range(hg):
            ls = slice(h * hd, (h + 1) * hd)
            vb = vb_ref[pl.ds(r0, kbw), ls]
            o = jnp.dot(p_ref[h], vb, preferred_element_type=F32)
            o_ref[pl.ds(r0, qb), ls] = o.astype(o_ref.dtype)
        return carry

    lax.fori_loop(0, tq // qb, block_body, 0, unroll=True)


def _attn_core(q, k, v, bias, rows_per_batch):
    n, d = q.shape
    heads = d // ATTN_HEAD_DIM
    tq = LOOKBACK_CHUNKS * CHUNK
    nt = rows_per_batch // tq
    nbat = n // rows_per_batch
    hg = _tile(heads, 8, 1)
    gwid = hg * ATTN_HEAD_DIM
    kbw = tq + ATTN_QCHUNKS * CHUNK
    cur = lambda b, t, g: (b * nt + t, g)
    prev = lambda b, t, g: (b * nt + jnp.maximum(t - 1, 0), g)
    vmem = 2 * (6 * _nbytes((tq, gwid), BF16) + _nbytes((hg, 2, CHUNK, ATTN_WINDOW), F32))
    vmem += 2 * _nbytes((2 * tq, gwid), BF16) + 4 * hg * _nbytes((ATTN_QCHUNKS * CHUNK, kbw), F32)
    return pl.pallas_call(
        functools.partial(_attn_kernel, tq=tq, hg=hg),
        out_shape=jax.ShapeDtypeStruct((n, d), BF16),
        grid=(nbat, nt, heads // hg),
        in_specs=[pl.BlockSpec((tq, gwid), cur),
                  pl.BlockSpec((tq, gwid), prev),
                  pl.BlockSpec((tq, gwid), cur),
                  pl.BlockSpec((tq, gwid), prev),
                  pl.BlockSpec((tq, gwid), cur),
                  pl.BlockSpec((hg, 2, CHUNK, ATTN_WINDOW), lambda b, t, g: (g, 0, 0, 0))],
        out_specs=pl.BlockSpec((tq, gwid), cur),
        scratch_shapes=[pltpu.VMEM((2 * tq, gwid), BF16),
                        pltpu.VMEM((2 * tq, gwid), BF16),
                        pltpu.VMEM((hg, ATTN_QCHUNKS * CHUNK, kbw), F32),
                        pltpu.VMEM((hg, ATTN_QCHUNKS * CHUNK, kbw), BF16)],
        compiler_params=_params(("parallel", "parallel", "parallel"), vmem),
        name="band_attn",
    )(q, k, k, v, v, bias)


def _attn_layer(x, h, rows_per_batch, layer, wq, wk, wv, wo, rel_table):
    q = _matmul(h, wq, layer=layer, scale=ATTN_HEAD_DIM ** -0.5, out_dtype=BF16, name="attn_q")
    k = _matmul(h, wk, layer=layer, out_dtype=BF16, name="attn_k")
    v = _matmul(h, wv, layer=layer, out_dtype=BF16, name="attn_v")
    o = _attn_core(q, k, v, _rel_bias(rel_table), rows_per_batch)
    return _matmul(o, wo, layer=layer, residual=x, out_dtype=F32, name="attn_out")


def _ffn(x, h, layer, w_gate, w_up, w_down):
    act = _swiglu_up(h, w_gate, w_up, layer)
    return _matmul(act, w_down, layer=layer, residual=x, out_dtype=F32, bm=512, bn=512, name="ffn_down")


def kernel(x, mem, norm_mix, norm_mem_q, norm_mem_kv, norm_ffn, norm_final, mem_wq, mem_wk, mem_wv, mem_wo, ffn_w_gate, ffn_w_up, ffn_w_down, lru_w_in, lru_b_in, lru_w_gate, lru_b_gate, lru_conv_w, lru_conv_b, lru_w_a, lru_b_a, lru_w_x, lru_b_x, lru_lambda, lru_w_out, lru_b_out, ssd_w_in, ssd_conv_w, ssd_conv_b, ssd_dt_bias, ssd_a_log, ssd_d, ssd_norm, ssd_w_out, attn_wq, attn_wk, attn_wv, attn_wo, attn_rel_bias):
    nbat, seq, d = x.shape
    tm = mem.shape[1]
    depth = norm_mix.shape[0]
    xf = x.reshape(nbat * seq, d)
    memf = mem.reshape(nbat * tm, d)
    bf = lambda w: w.astype(BF16)
    lru_wi, lru_wg, lru_wo = bf(lru_w_in), bf(lru_w_gate), bf(lru_w_out)
    ssd_wi, ssd_wo = bf(ssd_w_in), bf(ssd_w_out)
    a_wq, a_wk, a_wv, a_wo = bf(attn_wq), bf(attn_wk), bf(attn_wv), bf(attn_wo)
    m_wq, m_wo = bf(mem_wq), bf(mem_wo)
    m_wkv = bf(jnp.concatenate([mem_wk, mem_wv], axis=2))
    ffn_wd = bf(ffn_w_down)
    for i in range(depth):
        kind, j = i % N_MIXERS, i // N_MIXERS
        h = _rmsnorm(xf, norm_mix[i], BF16)
        if kind == 0:
            xf = _lru_layer(xf, h, seq, j, lru_wi, lru_b_in[j], lru_wg, lru_b_gate[j],
                            lru_conv_w[j], lru_conv_b[j], lru_w_a[j], lru_b_a[j], lru_w_x[j], lru_b_x[j],
                            lru_lambda[j], lru_wo, lru_b_out[j])
        elif kind == 1:
            xf = _ssd_layer(xf, h, seq, j, ssd_wi, ssd_w_in, ssd_conv_w[j], ssd_conv_b[j], ssd_dt_bias[j],
                            ssd_a_log[j], ssd_d[j], ssd_norm[j], ssd_wo)
        else:
            xf = _attn_layer(xf, h, seq, j, a_wq, a_wk, a_wv, a_wo, attn_rel_bias[j])
        hm = _rmsnorm(memf, norm_mem_kv[i], BF16)
        kv = _matmul(hm, m_wkv, layer=i, out_dtype=BF16, name="mem_kv").reshape(nbat, tm, -1)
        xf, h2 = _mem_xattn(xf, seq, kv, norm_mem_q[i], m_wq, m_wo, i, norm_ffn[i])
        xf = _ffn(xf, h2, i, ffn_w_gate, ffn_w_up, ffn_wd)
    return _rmsnorm(xf, norm_final, F32).reshape(nbat, seq, d)
```

```python
import functools

import numpy as np
import jax
import jax.numpy as jnp
from jax import lax
from jax.experimental import pallas as pl
from jax.experimental.pallas import tpu as pltpu

F32 = jnp.float32
BF16 = jnp.bfloat16

V7X_LANES = 128
V7X_SUBLANES = 8
V7X_VMEM_BYTES = 64 * 2**20
V7X_VMEM_LIMIT_CAP = V7X_VMEM_BYTES - 6 * 2**20
COMPILER_SCRATCH_ALLOWANCE = 4 * 2**20

RMS_EPS = 1e-6
N_MIXERS = 3
CHUNK = 64
LRU_C = 8.0
LRU_CONV = 4
SSD_HEAD_DIM = 64
SSD_STATE = 128
SSD_GROUPS = 8
SSD_CONV = 4
SSD_CONV_LANES = 4 * V7X_LANES
ATTN_HEAD_DIM = 128
LOOKBACK_CHUNKS = 8
MAX_REL_DIST = 256
ATTN_QCHUNKS = 4
ATTN_WINDOW = (LOOKBACK_CHUNKS + 2) * CHUNK
MEM_HEADS = 4
MEM_HEAD_DIM = 128
CONV_PAD_ROWS = V7X_SUBLANES
ROW_BLOCK = V7X_SUBLANES * V7X_SUBLANES


def _tile(dim, target, align):
    if dim <= target:
        return dim
    best = None
    for t in range(align, target + 1, align):
        if dim % t == 0:
            best = t
    assert best is not None, (dim, target, align)
    return best


def _params(semantics, vmem_bytes):
    limit = vmem_bytes + COMPILER_SCRATCH_ALLOWANCE
    return pltpu.CompilerParams(
        dimension_semantics=semantics,
        vmem_limit_bytes=int(min(max(limit, 16 * 2**20), V7X_VMEM_LIMIT_CAP)),
    )


def _nbytes(shape, dtype):
    return int(np.prod(shape)) * jnp.dtype(dtype).itemsize


def _rms(x, g):
    ms = jnp.mean(x * x, axis=-1, keepdims=True)
    return x * lax.rsqrt(ms + RMS_EPS) * g


def _silu(x):
    hx = 0.5 * x
    return hx * jnp.tanh(hx) + hx


def _softmax_rows(s):
    m = jnp.max(s, axis=-1, keepdims=True)
    e = jnp.exp(s - m)
    return e / jnp.sum(e, axis=-1, keepdims=True)


def _rmsnorm_kernel(x_ref, g_ref, o_ref):
    o_ref[...] = _rms(x_ref[...], g_ref[...]).astype(o_ref.dtype)


def _rmsnorm(x, g, out_dtype):
    n, d = x.shape
    bm = _tile(n, 512, V7X_SUBLANES)
    vmem = 2 * (_nbytes((bm, d), F32) + _nbytes((bm, d), out_dtype)) + 2 * _nbytes((bm, d), F32)
    return pl.pallas_call(
        _rmsnorm_kernel,
        out_shape=jax.ShapeDtypeStruct((n, d), out_dtype),
        grid=(n // bm,),
        in_specs=[pl.BlockSpec((bm, d), lambda i: (i, 0)),
                  pl.BlockSpec((1, d), lambda i: (0, 0))],
        out_specs=pl.BlockSpec((bm, d), lambda i: (i, 0)),
        compiler_params=_params(("parallel",), vmem),
        name="rmsnorm",
    )(x, g.reshape(1, d))


def _mm_kernel(x_ref, w_ref, *rest, has_bias, has_res, act, scale):
    rest = list(rest)
    b_ref = rest.pop(0) if has_bias else None
    r_ref = rest.pop(0) if has_res else None
    (o_ref,) = rest
    acc = jnp.dot(x_ref[...], w_ref[...], preferred_element_type=F32)
    if has_bias:
        acc = acc + b_ref[...]
    if scale is not None:
        acc = acc * scale
    if act == "gelu":
        acc = jax.nn.gelu(acc)
    if has_res:
        acc = acc + r_ref[...]
    o_ref[...] = acc.astype(o_ref.dtype)


def _matmul(x, w, *, layer=0, col0=0, ncols=None, bias=None, residual=None, act=None, scale=None,
            out_dtype=F32, bm=1024, bn=1024, name="matmul"):
    m, kdim = x.shape
    n = w.shape[2] - col0 if ncols is None else ncols
    bm = _tile(m, bm, V7X_SUBLANES * 2)
    bn = _tile(n, bn, V7X_LANES)
    assert col0 % bn == 0 and w.shape[1] == kdim
    jb0 = col0 // bn
    in_specs = [pl.BlockSpec((bm, kdim), lambda i, j: (i, 0)),
                pl.BlockSpec((None, kdim, bn), lambda i, j: (layer, 0, jb0 + j))]
    args = [x, w]
    vmem = 2 * (_nbytes((bm, kdim), x.dtype) + _nbytes((kdim, bn), w.dtype) + _nbytes((bm, bn), out_dtype))
    vmem += (2 + 2 * (act is not None)) * _nbytes((bm, bn), F32)
    if bias is not None:
        in_specs.append(pl.BlockSpec((1, bn), lambda i, j: (0, j)))
        args.append(bias.reshape(1, n).astype(F32))
    if residual is not None:
        in_specs.append(pl.BlockSpec((bm, bn), lambda i, j: (i, j)))
        args.append(residual)
        vmem += 2 * _nbytes((bm, bn), residual.dtype)
    return pl.pallas_call(
        functools.partial(_mm_kernel, has_bias=bias is not None,
                          has_res=residual is not None, act=act, scale=scale),
        out_shape=jax.ShapeDtypeStruct((m, n), out_dtype),
        grid=(m // bm, n // bn),
        in_specs=in_specs,
        out_specs=pl.BlockSpec((bm, bn), lambda i, j: (i, j)),
        compiler_params=_params(("parallel", "parallel"), vmem),
        name=name,
    )(*args)


def _swiglu_up_kernel(x_ref, wg_ref, wu_ref, o_ref, wgb_ref, wub_ref):
    @pl.when(pl.program_id(1) == 0)
    def _():
        wgb_ref[...] = wg_ref[...].astype(BF16)
        wub_ref[...] = wu_ref[...].astype(BF16)

    x = x_ref[...]
    g = jnp.dot(x, wgb_ref[...], preferred_element_type=F32)
    u = jnp.dot(x, wub_ref[...], preferred_element_type=F32)
    o_ref[...] = (_silu(g) * u).astype(o_ref.dtype)


def _swiglu_up(x, wg, wu, layer, *, bm=1024, bn=256):
    m, kdim = x.shape
    n = wg.shape[2]
    bm = _tile(m, bm, V7X_SUBLANES * 2)
    bn = _tile(n, bn, V7X_LANES)
    vmem = 2 * (_nbytes((bm, kdim), BF16) + 2 * _nbytes((kdim, bn), F32) + _nbytes((bm, bn), BF16))
    vmem += 2 * _nbytes((kdim, bn), BF16) + 5 * _nbytes((bm, bn), F32) + 2 * _nbytes((kdim, bn), F32)
    wspec = pl.BlockSpec((None, kdim, bn), lambda j, i: (layer, 0, j))
    return pl.pallas_call(
        _swiglu_up_kernel,
        out_shape=jax.ShapeDtypeStruct((m, n), BF16),
        grid=(n // bn, m // bm),
        in_specs=[pl.BlockSpec((bm, kdim), lambda j, i: (i, 0)), wspec, wspec],
        out_specs=pl.BlockSpec((bm, bn), lambda j, i: (i, j)),
        scratch_shapes=[pltpu.VMEM((kdim, bn), BF16), pltpu.VMEM((kdim, bn), BF16)],
        compiler_params=_params(("parallel", "arbitrary"), vmem),
        name="swiglu_up",
    )(x, wg, wu)


def _memattn_kernel(x_ref, gq_ref, wq_ref, k_ref, v_ref, wo_ref, gf_ref, xo_ref, ho_ref):
    x = x_ref[...]
    h = _rms(x, gq_ref[...]).astype(BF16)
    q = jnp.dot(h, wq_ref[...], preferred_element_type=F32).astype(BF16)
    k = k_ref[0]
    v = v_ref[0]
    scale = MEM_HEAD_DIM ** -0.5
    outs = []
    for hh in range(MEM_HEADS):
        sl = slice(hh * MEM_HEAD_DIM, (hh + 1) * MEM_HEAD_DIM)
        s = lax.dot_general(q[:, sl], k[:, sl], (((1,), (1,)), ((), ())),
                            preferred_element_type=F32) * scale
        p = _softmax_rows(s).astype(BF16)
        outs.append(jnp.dot(p, v[:, sl], preferred_element_type=F32).astype(BF16))
    o = jnp.concatenate(outs, axis=1)
    x2 = x + jnp.dot(o, wo_ref[...], preferred_element_type=F32)
    xo_ref[...] = x2
    ho_ref[...] = _rms(x2, gf_ref[...]).astype(BF16)


def _mem_xattn(x, rows_per_batch, kv, g_q, wq, wo, layer, g_ffn):
    n, d = x.shape
    nb, tm, width2 = kv.shape
    width = width2 // 2
    bm = _tile(rows_per_batch, 256, V7X_SUBLANES * 2)
    per_batch = rows_per_batch // bm
    vmem = 2 * (2 * _nbytes((bm, d), F32) + _nbytes((bm, d), BF16) + 2 * _nbytes((d, width), BF16)
                + 2 * _nbytes((tm, width), BF16))
    vmem += 4 * _nbytes((bm, d), F32)
    return pl.pallas_call(
        _memattn_kernel,
        out_shape=(jax.ShapeDtypeStruct((n, d), F32), jax.ShapeDtypeStruct((n, d), BF16)),
        grid=(n // bm,),
        in_specs=[pl.BlockSpec((bm, d), lambda i: (i, 0)),
                  pl.BlockSpec((1, d), lambda i: (0, 0)),
                  pl.BlockSpec((None, d, width), lambda i: (layer, 0, 0)),
                  pl.BlockSpec((1, tm, width), lambda i: (i // per_batch, 0, 0)),
                  pl.BlockSpec((1, tm, width), lambda i: (i // per_batch, 0, 1)),
                  pl.BlockSpec((None, width, d), lambda i: (layer, 0, 0)),
                  pl.BlockSpec((1, d), lambda i: (0, 0))],
        out_specs=(pl.BlockSpec((bm, d), lambda i: (i, 0)),
                   pl.BlockSpec((bm, d), lambda i: (i, 0))),
        compiler_params=_params(("parallel",), vmem),
        name="mem_xattn",
    )(x, g_q.reshape(1, d), wq, kv, kv, wo, g_ffn.reshape(1, d))


def _strided_rows(ref, slabs, row0, r):
    return jnp.concatenate(
        [ref[s, pl.ds(row0 + r, V7X_SUBLANES, stride=V7X_SUBLANES), :] for s in slabs], axis=1)


def _store_strided_rows(ref, slabs, row0, r, val):
    for k, s in enumerate(slabs):
        ref[s, pl.ds(row0 + r, V7X_SUBLANES, stride=V7X_SUBLANES), :] = val[:, k * V7X_LANES:(k + 1) * V7X_LANES]


def _conv_row_block(xs_ref, slabs, base, cw, cb, ntaps):
    nsl = V7X_SUBLANES
    ext = {r: _strided_rows(xs_ref, slabs, base, r) for r in range(nsl)}
    row = lax.broadcasted_iota(jnp.int32, ext[0].shape, 0)
    for k in range(1, ntaps):
        above = jnp.concatenate([xs_ref[s, pl.ds(base - k, 1), :] for s in slabs], axis=1)
        ext[-k] = jnp.where(row >= 1, pltpu.roll(ext[nsl - k], 1, axis=0), above)
    out = []
    for r in range(nsl):
        y = cb
        for j in range(ntaps):
            y = y + cw[j:j + 1, :] * ext[r - (ntaps - 1) + j]
        out.append(y)
    return out


def _lru_kernel(xin_ref, gate_ref, cw_ref, cb_ref, wax_ref, bax_ref, lam_ref, o_ref,
                xs_ref, xc_ref, a_ref, u_ref, h_ref, *, tb, nblk, blk):
    t = pl.program_id(2)
    pad = CONV_PAD_ROWS
    lanes = V7X_LANES
    nsl = V7X_SUBLANES
    nslab = nblk * blk // lanes
    per_blk = blk // lanes
    nrb = tb // ROW_BLOCK

    @pl.when(t == 0)
    def _():
        xs_ref[:, 0:pad, :] = jnp.zeros((nslab, pad, lanes), F32)
        h_ref[...] = jnp.zeros(h_ref.shape, F32)

    @pl.when(t > 0)
    def _():
        xs_ref[:, 0:pad, :] = xs_ref[:, tb:tb + pad, :]

    for s in range(nslab):
        xs_ref[s, pad:pad + tb, :] = xin_ref[:, s * lanes:(s + 1) * lanes]

    for nb in range(nblk):
        slabs = list(range(nb * per_blk, (nb + 1) * per_blk))
        sl = slice(nb * blk, (nb + 1) * blk)
        cw = cw_ref[:, sl]
        cb = cb_ref[:, sl]

        def conv_body(b, carry, slabs=slabs, cw=cw, cb=cb):
            r0 = pl.multiple_of(b * ROW_BLOCK, ROW_BLOCK)
            ys = _conv_row_block(xs_ref, slabs, pad + r0, cw, cb, LRU_CONV)
            for r in range(nsl):
                _store_strided_rows(xc_ref, slabs, r0, r, ys[r])
            return carry

        lax.fori_loop(0, nrb, conv_body, 0, unroll=2)

    half_neg_sp = (-0.5 * LRU_C) * jax.nn.softplus(-lam_ref[...])
    for nb in range(nblk):
        slabs = list(range(nb * per_blk, (nb + 1) * per_blk))
        sl = slice(nb * blk, (nb + 1) * blk)
        xcb = jnp.concatenate([xc_ref[s] for s in slabs], axis=1)
        half_gates = jnp.dot(xcb.astype(BF16), wax_ref[nb], preferred_element_type=F32) + bax_ref[nb]
        tr = jnp.tanh(half_gates[:, :blk])
        ig = 0.5 * jnp.tanh(half_gates[:, blk:]) + 0.5
        log_a = half_neg_sp[:, sl] * tr + half_neg_sp[:, sl]
        th = jnp.tanh(log_a)
        n = -2.0 * th
        root = jnp.where(n > 0.0, n * lax.rsqrt(n * (1.0 - th)), 0.0)
        a = jnp.exp(log_a)
        u = root * (ig * xcb)
        for k, s in enumerate(slabs):
            a_ref[s] = a[:, k * lanes:(k + 1) * lanes]
            u_ref[s] = u[:, k * lanes:(k + 1) * lanes]

    for nb in range(nblk):
        slabs = list(range(nb * per_blk, (nb + 1) * per_blk))
        sl = slice(nb * blk, (nb + 1) * blk)
        row = lax.broadcasted_iota(jnp.int32, (nsl, blk), 0)

        def scan_body(b, h, slabs=slabs, row=row):
            r0 = pl.multiple_of(b * ROW_BLOCK, ROW_BLOCK)
            acc_a, acc_h = [], []
            for r in range(nsl):
                a = _strided_rows(a_ref, slabs, r0, r)
                u = _strided_rows(u_ref, slabs, r0, r)
                if r > 0:
                    u = a * acc_h[-1] + u
                    a = a * acc_a[-1]
                acc_a.append(a)
                acc_h.append(u)
            ga, gh = acc_a[-1], acc_h[-1]
            for sh in (1, 2, 4):
                ga_prev = jnp.where(row >= sh, pltpu.roll(ga, sh, axis=0), 1.0)
                gh_prev = jnp.where(row >= sh, pltpu.roll(gh, sh, axis=0), 0.0)
                gh = ga * gh_prev + gh
                ga = ga * ga_prev
            ends = gh + ga * h
            enter = jnp.where(row >= 1, pltpu.roll(ends, 1, axis=0), h)
            for r in range(nsl):
                _store_strided_rows(u_ref, slabs, r0, r, acc_h[r] + acc_a[r] * enter)
            return ends[nsl - 1:nsl, :]

        h_ref[:, sl] = lax.fori_loop(0, nrb, scan_body, h_ref[:, sl], unroll=2)

    hs = jnp.concatenate([u_ref[s] for s in range(nslab)], axis=1)
    o_ref[...] = (hs * gate_ref[...].astype(F32)).astype(o_ref.dtype)


def _lru_core(xin, gate, rows_per_batch, conv_w, conv_b, w_ax, b_ax, lam):
    n, r = xin.shape
    nblocks, blk = w_ax.shape[0], w_ax.shape[1]
    nblk = _tile(nblocks, 3, 1)
    c = nblk * blk
    tb = _tile(rows_per_batch, 512, ROW_BLOCK)
    nt = rows_per_batch // tb
    nbat = n // rows_per_batch
    pad = CONV_PAD_ROWS
    nslab = c // V7X_LANES
    assert blk % V7X_LANES == 0 and tb % ROW_BLOCK == 0
    vmem = 2 * (_nbytes((tb, c), F32) + 2 * _nbytes((tb, c), BF16) + _nbytes((nblk, blk, 2 * blk), BF16))
    vmem += _nbytes((tb + pad, c), F32) + 3 * _nbytes((tb, c), F32) + 6 * _nbytes((tb, c), F32)
    return pl.pallas_call(
        functools.partial(_lru_kernel, tb=tb, nblk=nblk, blk=blk),
        out_shape=jax.ShapeDtypeStruct((n, r), BF16),
        grid=(nbat, r // c, nt),
        in_specs=[pl.BlockSpec((tb, c), lambda b, j, t: (b * nt + t, j)),
                  pl.BlockSpec((tb, c), lambda b, j, t: (b * nt + t, j)),
                  pl.BlockSpec((LRU_CONV, c), lambda b, j, t: (0, j)),
                  pl.BlockSpec((1, c), lambda b, j, t: (0, j)),
                  pl.BlockSpec((nblk, blk, 2 * blk), lambda b, j, t: (j, 0, 0)),
                  pl.BlockSpec((nblk, 1, 2 * blk), lambda b, j, t: (j, 0, 0)),
                  pl.BlockSpec((1, c), lambda b, j, t: (0, j))],
        out_specs=pl.BlockSpec((tb, c), lambda b, j, t: (b * nt + t, j)),
        scratch_shapes=[pltpu.VMEM((nslab, tb + pad, V7X_LANES), F32),
                        pltpu.VMEM((nslab, tb, V7X_LANES), F32),
                        pltpu.VMEM((nslab, tb, V7X_LANES), F32),
                        pltpu.VMEM((nslab, tb, V7X_LANES), F32),
                        pltpu.VMEM((1, c), F32)],
        compiler_params=_params(("parallel", "parallel", "arbitrary"), vmem),
        name="lru_core",
    )(xin, gate, conv_w, conv_b.reshape(1, r), w_ax, b_ax, lam.reshape(1, r))


def _lru_layer(x, h, rows_per_batch, layer, w_in, b_in, w_gate, b_gate, conv_w, conv_b,
               w_a, b_a, w_x, b_x, lam, w_out, b_out):
    nblocks, blk = w_a.shape[0], w_a.shape[1]
    gate = _matmul(h, w_gate, layer=layer, bias=b_gate, act="gelu", out_dtype=BF16,
                   bn=768, name="lru_gate")
    xin = _matmul(h, w_in, layer=layer, bias=b_in, out_dtype=F32, bn=768, name="lru_in")
    w_ax = (0.5 * jnp.concatenate([w_a, w_x], axis=-1)).astype(BF16)
    b_ax = 0.5 * jnp.concatenate([b_a.reshape(nblocks, 1, blk), b_x.reshape(nblocks, 1, blk)], axis=-1)
    hs = _lru_core(xin, gate, rows_per_batch, conv_w, conv_b, w_ax, b_ax, lam)
    return _matmul(hs, w_out, layer=layer, bias=b_out, residual=x, out_dtype=F32,
                   bm=512, name="lru_out")


def _split3(v):
    hi = v.astype(BF16)
    r1 = v - hi.astype(F32)
    mid = r1.astype(BF16)
    lo = (r1 - mid.astype(F32)).astype(BF16)
    return hi, mid, lo


def _ssd_kernel(z_ref, xbc_ref, dtr_ref, cw_ref, cb_ref, dtb_ref, alog_ref, dsk_ref, nw_ref, y_ref,
                xs_ref, xa_ref, st_ref, ct2_ref, dt2_ref, wt2_ref, *, d_inner):
    c = pl.program_id(1)
    L = CHUNK
    pad = CONV_PAD_ROWS
    lanes = V7X_LANES
    half = lanes // 2
    gw = d_inner // SSD_GROUPS
    pairs_per_group = gw // lanes
    npair = d_inner // lanes
    assert SSD_HEAD_DIM == half and SSD_STATE == lanes and 2 * npair == lanes and L == half

    nslab = xs_ref.shape[0]
    slabs_per_pass = SSD_CONV_LANES // lanes
    assert L == ROW_BLOCK

    @pl.when(c == 0)
    def _():
        xs_ref[:, 0:pad, :] = jnp.zeros((nslab, pad, lanes), F32)
        st_ref[...] = jnp.zeros(st_ref.shape, F32)

    @pl.when(c > 0)
    def _():
        xs_ref[:, 0:pad, :] = xs_ref[:, L:L + pad, :]

    def conv_body(i, carry):
        l0 = pl.multiple_of(i * SSD_CONV_LANES, SSD_CONV_LANES)
        slabs = [i * slabs_per_pass + k for k in range(slabs_per_pass)]
        for k, s in enumerate(slabs):
            xs_ref[s, pad:pad + L, :] = xbc_ref[:, pl.ds(l0 + k * lanes, lanes)]
        ys = _conv_row_block(xs_ref, slabs, pad, cw_ref[:, pl.ds(l0, SSD_CONV_LANES)],
                             cb_ref[:, pl.ds(l0, SSD_CONV_LANES)], SSD_CONV)
        for r in range(V7X_SUBLANES):
            _store_strided_rows(xa_ref, slabs, 0, r, _silu(ys[r]))
        return carry

    lax.fori_loop(0, nslab // slabs_per_pass, conv_body, 0, unroll=2)

    dt = jax.nn.softplus(dtr_ref[...] + dtb_ref[...])
    da = dt * (-jnp.exp(alog_ref[...]))
    ri = lax.broadcasted_iota(jnp.int32, (L, L), 0)
    ci = lax.broadcasted_iota(jnp.int32, (L, L), 1)
    tril = jnp.where(ri >= ci, 1.0, 0.0).astype(BF16)
    parts = jnp.dot(tril, jnp.concatenate(_split3(da), axis=1), preferred_element_type=F32)
    cum = (parts[:, :lanes] + parts[:, lanes:2 * lanes]) + parts[:, 2 * lanes:]
    last = cum[L - 1:L, :]
    w_end = dt * jnp.exp(last - cum)
    dec_chunk = jnp.exp(last)

    def pair_rows(v):
        vt = v.T
        return jnp.concatenate([vt[:npair], vt[npair:]], axis=1)

    ct2_ref[...] = pair_rows(cum)
    dt2_ref[...] = pair_rows(dt)
    wt2_ref[...] = pair_rows(w_end)

    lane_i = lax.broadcasted_iota(jnp.int32, (L, lanes), 1)
    row_i = lax.broadcasted_iota(jnp.int32, (L, lanes), 0)
    lo_half = lane_i < half
    causal2 = row_i >= jnp.where(lo_half, lane_i, lane_i - half)
    lo_half_row = lax.broadcasted_iota(jnp.int32, (1, lanes), 1) < half
    lo_half_sq = lax.broadcasted_iota(jnp.int32, (lanes, lanes), 1) < half
    top_half_sq = lax.broadcasted_iota(jnp.int32, (lanes, lanes), 0) < half
    blockdiag = lo_half_sq == top_half_sq

    def group_body(g, carry):
        x0 = pl.multiple_of(g * gw, gw)
        p0 = pl.multiple_of(g * pairs_per_group, pairs_per_group)
        bg = xa_ref[npair + g]
        cg = xa_ref[npair + SSD_GROUPS + g].astype(BF16)
        b2 = jnp.concatenate([bg, bg], axis=0)
        cb2 = lax.dot_general(cg, b2.astype(BF16), (((1,), (1,)), ((), ())),
                              preferred_element_type=F32)
        bt2 = b2.T
        sg = st_ref[g]
        yoff = jnp.dot(cg, sg.astype(BF16), preferred_element_type=F32)
        shift = lax.rem(lanes - g * pairs_per_group, lanes)
        cum_r = pltpu.roll(cum, shift, axis=1)
        dec_r = pltpu.roll(dec_chunk, shift, axis=1)
        ct8 = ct2_ref[pl.ds(p0, pairs_per_group), :]
        dt8 = dt2_ref[pl.ds(p0, pairs_per_group), :]
        wt8 = wt2_ref[pl.ds(p0, pairs_per_group), :]
        xps = [xa_ref[p0 + pj] for pj in range(pairs_per_group)]
        xg = jnp.concatenate(xps, axis=1)
        ys, snew = [], []
        for pj in range(pairs_per_group):
            ls = slice(pj * lanes, (pj + 1) * lanes)
            xp = xps[pj]
            colp = jnp.where(lo_half,
                             jnp.broadcast_to(cum_r[:, pj:pj + 1], (L, lanes)),
                             jnp.broadcast_to(cum_r[:, half + pj:half + pj + 1], (L, lanes)))
            seg = jnp.where(causal2, colp - ct8[pj:pj + 1, :], -jnp.inf)
            m_diag = (cb2 * jnp.exp(seg) * dt8[pj:pj + 1, :]).astype(BF16)
            m_state = (bt2 * wt8[pj:pj + 1, :]).astype(BF16)
            x2 = jnp.concatenate([xp, xp], axis=0)
            xbd = jnp.where(blockdiag, x2, 0.0).astype(BF16)
            res = jnp.dot(jnp.concatenate([m_diag, m_state], axis=0), xbd,
                          preferred_element_type=F32)
            ys.append(res[:L] + jnp.exp(colp) * yoff[:, ls])
            decp = jnp.where(lo_half_row,
                             jnp.broadcast_to(dec_r[:, pj:pj + 1], (1, lanes)),
                             jnp.broadcast_to(dec_r[:, half + pj:half + pj + 1], (1, lanes)))
            snew.append(decp * sg[:, ls] + res[L:])
        st_ref[g] = jnp.concatenate(snew, axis=1)
        y = jnp.concatenate(ys, axis=1) + dsk_ref[:, pl.ds(x0, gw)] * xg
        y = y * _silu(z_ref[:, pl.ds(x0, gw)].astype(F32))
        y = _rms(y, nw_ref[:, pl.ds(x0, gw)])
        y_ref[:, pl.ds(x0, gw)] = y.astype(y_ref.dtype)
        return carry

    lax.fori_loop(0, SSD_GROUPS, group_body, 0, unroll=2)


def _ssd_core(z, xbc, dtr, rows_per_batch, conv_w, conv_b, dt_bias, a_log, d_rep, norm_w):
    n, d_inner = z.shape
    cdim = xbc.shape[1]
    nheads = dtr.shape[1]
    nc = rows_per_batch // CHUNK
    nbat = n // rows_per_batch
    gw = d_inner // SSD_GROUPS
    npair = d_inner // V7X_LANES
    pad = CONV_PAD_ROWS
    assert cdim % SSD_CONV_LANES == 0
    vmem = 2 * (2 * _nbytes((CHUNK, d_inner), BF16) + _nbytes((CHUNK, cdim), F32)
                + _nbytes((SSD_CONV + 1, cdim), F32))
    vmem += _nbytes((CHUNK + pad, cdim), F32) + _nbytes((CHUNK, cdim), F32)
    vmem += _nbytes((SSD_GROUPS, SSD_STATE, gw), F32) + 8 * _nbytes((SSD_STATE, gw), F32)
    row = lambda b, c: (b * nc + c, 0)
    const = lambda b, c: (0, 0)
    return pl.pallas_call(
        functools.partial(_ssd_kernel, d_inner=d_inner),
        out_shape=jax.ShapeDtypeStruct((n, d_inner), BF16),
        grid=(nbat, nc),
        in_specs=[pl.BlockSpec((CHUNK, d_inner), row),
                  pl.BlockSpec((CHUNK, cdim), row),
                  pl.BlockSpec((CHUNK, nheads), row),
                  pl.BlockSpec((SSD_CONV, cdim), const),
                  pl.BlockSpec((1, cdim), const),
                  pl.BlockSpec((1, nheads), const),
                  pl.BlockSpec((1, nheads), const),
                  pl.BlockSpec((1, d_inner), const),
                  pl.BlockSpec((1, d_inner), const)],
        out_specs=pl.BlockSpec((CHUNK, d_inner), row),
        scratch_shapes=[pltpu.VMEM((cdim // V7X_LANES, CHUNK + pad, V7X_LANES), F32),
                        pltpu.VMEM((cdim // V7X_LANES, CHUNK, V7X_LANES), F32),
                        pltpu.VMEM((SSD_GROUPS, SSD_STATE, gw), F32),
                        pltpu.VMEM((npair, V7X_LANES), F32),
                        pltpu.VMEM((npair, V7X_LANES), F32),
                        pltpu.VMEM((npair, V7X_LANES), F32)],
        compiler_params=_params(("parallel", "arbitrary"), vmem),
        name="ssd_core",
    )(z, xbc, dtr, conv_w, conv_b.reshape(1, cdim), dt_bias.reshape(1, nheads),
      a_log.reshape(1, nheads), d_rep.reshape(1, d_inner), norm_w.reshape(1, d_inner))


def _ssd_layer(x, h, rows_per_batch, layer, w_in, w_in_f32, conv_w, conv_b, dt_bias, a_log, d_skip,
               norm_w, w_out):
    nheads = dt_bias.shape[0]
    d_inner = nheads * SSD_HEAD_DIM
    cdim = conv_w.shape[1]
    perm = np.concatenate([np.arange(0, nheads, 2), np.arange(1, nheads, 2)])
    w_dt = w_in_f32[layer, :, d_inner + cdim:][:, perm].astype(BF16)[None]
    z = _matmul(h, w_in, layer=layer, ncols=d_inner, out_dtype=BF16, name="ssd_in_z")
    xbc = _matmul(h, w_in, layer=layer, col0=d_inner, ncols=cdim, out_dtype=F32, name="ssd_in_xbc")
    dtr = _matmul(h, w_dt, out_dtype=F32, name="ssd_in_dt")
    d_rep = jnp.repeat(d_skip.astype(F32), SSD_HEAD_DIM)
    y = _ssd_core(z, xbc, dtr, rows_per_batch, conv_w, conv_b, dt_bias[perm], a_log[perm], d_rep, norm_w)
    return _matmul(y, w_out, layer=layer, residual=x, out_dtype=F32, bm=512, bn=512, name="ssd_out")


def _bias_gather_kernel(idx_ref, tab_ref, o_ref, *, heads):
    idx = idx_ref[...]
    ntab = tab_ref.shape[1]
    rows = lax.broadcasted_iota(jnp.int32, (ntab, idx.shape[1]), 0)
    onehot = jnp.where(rows == idx, 1.0, 0.0).astype(BF16)
    r = jnp.dot(tab_ref[...], onehot, preferred_element_type=F32)
    o_ref[...] = (r[0:heads] + r[heads:2 * heads]) + r[2 * heads:3 * heads]


def _rel_bias(rel_table):
    heads, ntab = rel_table.shape
    pad = LOOKBACK_CHUNKS * CHUNK
    band = pad + CHUNK
    rel = (np.arange(CHUNK)[:, None] + pad) - np.arange(band)[None, :]
    idx = (np.clip(rel, -MAX_REL_DIST, MAX_REL_DIST) + MAX_REL_DIST).astype(np.int32).reshape(1, -1)
    npos = idx.shape[1]
    ntab_p = -(-ntab // V7X_LANES) * V7X_LANES
    tab = jnp.pad(rel_table.astype(F32), ((0, 0), (0, ntab_p - ntab)))
    tab3 = jnp.concatenate(_split3(tab), axis=0)
    tp = _tile(npos, 2048, V7X_LANES)
    out = pl.pallas_call(
        functools.partial(_bias_gather_kernel, heads=heads),
        out_shape=jax.ShapeDtypeStruct((heads, npos), F32),
        grid=(npos // tp,),
        in_specs=[pl.BlockSpec((1, tp), lambda i: (0, i)),
                  pl.BlockSpec((3 * heads, ntab_p), lambda i: (0, 0))],
        out_specs=pl.BlockSpec((heads, tp), lambda i: (0, i)),
        compiler_params=_params(("parallel",), 16 * 2**20),
        name="rel_bias_gather",
    )(jnp.asarray(idx), tab3)
    bias = out.reshape(heads, CHUNK, band)
    masked = jnp.full((heads, CHUNK, CHUNK), -jnp.inf, F32)
    return jnp.stack([jnp.concatenate([bias, masked], axis=2),
                      jnp.concatenate([masked, bias], axis=2)], axis=1)


def _attn_kernel(q_ref, kp_ref, kc_ref, vp_ref, vc_ref, bias_ref, o_ref, kb_ref, vb_ref, s_ref, p_ref,
                 *, tq, hg):
    t = pl.program_id(1)
    pad = LOOKBACK_CHUNKS * CHUNK
    hd = ATTN_HEAD_DIM
    qb = ATTN_QCHUNKS * CHUNK
    kbw = pad + qb
    assert tq == pad and ATTN_QCHUNKS % 2 == 0 and ATTN_WINDOW == pad + 2 * CHUNK
    kb_ref[0:pad, :] = kp_ref[...]
    kb_ref[pad:pad + tq, :] = kc_ref[...]
    vb_ref[0:pad, :] = vp_ref[...]
    vb_ref[pad:pad + tq, :] = vc_ref[...]
    col = lax.broadcasted_iota(jnp.int32, (CHUNK, ATTN_WINDOW), 1)

    def block_body(sb, carry):
        r0 = pl.multiple_of(sb * qb, qb)
        first_key = t * tq + sb * qb - pad
        for h in range(hg):
            ls = slice(h * hd, (h + 1) * hd)
            qh = q_ref[pl.ds(r0, qb), ls]
            kb = kb_ref[pl.ds(r0, kbw), ls]
            s_ref[h] = lax.dot_general(qh, kb, (((1,), (1,)), ((), ())), preferred_element_type=F32)
        for h in range(hg):
            for c in range(ATTN_QCHUNKS):
                w0 = (c // 2) * 2 * CHUNK
                rows = slice(c * CHUNK, (c + 1) * CHUNK)
                sc = s_ref[h, rows, w0:w0 + ATTN_WINDOW] + bias_ref[h, c % 2]
                sc = jnp.where(first_key + w0 + col >= 0, sc, -jnp.inf)
                p_ref[h, rows, w0:w0 + ATTN_WINDOW] = _softmax_rows(sc).astype(BF16)
                if w0 > 0:
                    p_ref[h, rows, 0:w0] = jnp.zeros((CHUNK, w0), BF16)
                if w0 + ATTN_WINDOW < kbw:
                    p_ref[h, rows, w0 + ATTN_WINDOW:kbw] = jnp.zeros((CHUNK, kbw - w0 - ATTN_WINDOW), BF16)
        for h in range(hg):
            ls = slice(h * hd, (h + 1) * hd)
            vb = vb_ref[pl.ds(r0, kbw), ls]
            o = jnp.dot(p_ref[h], vb, preferred_element_type=F32)
            o_ref[pl.ds(r0, qb), ls] = o.astype(o_ref.dtype)
        return carry

    lax.fori_loop(0, tq // qb, block_body, 0, unroll=True)


def _attn_core(q, k, v, bias, rows_per_batch):
    n, d = q.shape
    heads = d // ATTN_HEAD_DIM
    tq = LOOKBACK_CHUNKS * CHUNK
    nt = rows_per_batch // tq
    nbat = n // rows_per_batch
    hg = _tile(heads, 8, 1)
    gwid = hg * ATTN_HEAD_DIM
    kbw = tq + ATTN_QCHUNKS * CHUNK
    cur = lambda b, t, g: (b * nt + t, g)
    prev = lambda b, t, g: (b * nt + jnp.maximum(t - 1, 0), g)
    vmem = 2 * (6 * _nbytes((tq, gwid), BF16) + _nbytes((hg, 2, CHUNK, ATTN_WINDOW), F32))
    vmem += 2 * _nbytes((2 * tq, gwid), BF16) + 4 * hg * _nbytes((ATTN_QCHUNKS * CHUNK, kbw), F32)
    return pl.pallas_call(
        functools.partial(_attn_kernel, tq=tq, hg=hg),
        out_shape=jax.ShapeDtypeStruct((n, d), BF16),
        grid=(nbat, nt, heads // hg),
        in_specs=[pl.BlockSpec((tq, gwid), cur),
                  pl.BlockSpec((tq, gwid), prev),
                  pl.BlockSpec((tq, gwid), cur),
                  pl.BlockSpec((tq, gwid), prev),
                  pl.BlockSpec((tq, gwid), cur),
                  pl.BlockSpec((hg, 2, CHUNK, ATTN_WINDOW), lambda b, t, g: (g, 0, 0, 0))],
        out_specs=pl.BlockSpec((tq, gwid), cur),
        scratch_shapes=[pltpu.VMEM((2 * tq, gwid), BF16),
                        pltpu.VMEM((2 * tq, gwid), BF16),
                        pltpu.VMEM((hg, ATTN_QCHUNKS * CHUNK, kbw), F32),
                        pltpu.VMEM((hg, ATTN_QCHUNKS * CHUNK, kbw), BF16)],
        compiler_params=_params(("parallel", "parallel", "parallel"), vmem),
        name="band_attn",
    )(q, k, k, v, v, bias)


def _attn_layer(x, h, rows_per_batch, layer, wq, wk, wv, wo, rel_table):
    q = _matmul(h, wq, layer=layer, scale=ATTN_HEAD_DIM ** -0.5, out_dtype=BF16, name="attn_q")
    k = _matmul(h, wk, layer=layer, out_dtype=BF16, name="attn_k")
    v = _matmul(h, wv, layer=layer, out_dtype=BF16, name="attn_v")
    o = _attn_core(q, k, v, _rel_bias(rel_table), rows_per_batch)
    return _matmul(o, wo, layer=layer, residual=x, out_dtype=F32, name="attn_out")


def _ffn(x, h, layer, w_gate, w_up, w_down):
    act = _swiglu_up(h, w_gate, w_up, layer)
    return _matmul(act, w_down, layer=layer, residual=x, out_dtype=F32, bm=512, bn=512, name="ffn_down")


def kernel(x, mem, norm_mix, norm_mem_q, norm_mem_kv, norm_ffn, norm_final, mem_wq, mem_wk, mem_wv, mem_wo, ffn_w_gate, ffn_w_up, ffn_w_down, lru_w_in, lru_b_in, lru_w_gate, lru_b_gate, lru_conv_w, lru_conv_b, lru_w_a, lru_b_a, lru_w_x, lru_b_x, lru_lambda, lru_w_out, lru_b_out, ssd_w_in, ssd_conv_w, ssd_conv_b, ssd_dt_bias, ssd_a_log, ssd_d, ssd_norm, ssd_w_out, attn_wq, attn_wk, attn_wv, attn_wo, attn_rel_bias):
    nbat, seq, d = x.shape
    tm = mem.shape[1]
    depth = norm_mix.shape[0]
    xf = x.reshape(nbat * seq, d)
    memf = mem.reshape(nbat * tm, d)
    bf = lambda w: w.astype(BF16)
    lru_wi, lru_wg, lru_wo = bf(lru_w_in), bf(lru_w_gate), bf(lru_w_out)
    ssd_wi, ssd_wo = bf(ssd_w_in), bf(ssd_w_out)
    a_wq, a_wk, a_wv, a_wo = bf(attn_wq), bf(attn_wk), bf(attn_wv), bf(attn_wo)
    m_wq, m_wo = bf(mem_wq), bf(mem_wo)
    m_wkv = bf(jnp.concatenate([mem_wk, mem_wv], axis=2))
    ffn_wd = bf(ffn_w_down)
    for i in range(depth):
        kind, j = i % N_MIXERS, i // N_MIXERS
        h = _rmsnorm(xf, norm_mix[i], BF16)
        if kind == 0:
            xf = _lru_layer(xf, h, seq, j, lru_wi, lru_b_in[j], lru_wg, lru_b_gate[j],
                            lru_conv_w[j], lru_conv_b[j], lru_w_a[j], lru_b_a[j], lru_w_x[j], lru_b_x[j],
                            lru_lambda[j], lru_wo, lru_b_out[j])
        elif kind == 1:
            xf = _ssd_layer(xf, h, seq, j, ssd_wi, ssd_w_in, ssd_conv_w[j], ssd_conv_b[j], ssd_dt_bias[j],
                            ssd_a_log[j], ssd_d[j], ssd_norm[j], ssd_wo)
        else:
            xf = _attn_layer(xf, h, seq, j, a_wq, a_wk, a_wv, a_wo, attn_rel_bias[j])
        hm = _rmsnorm(memf, norm_mem_kv[i], BF16)
        kv = _matmul(hm, m_wkv, layer=i, out_dtype=BF16, name="mem_kv").reshape(nbat, tm, -1)
        xf, h2 = _mem_xattn(xf, seq, kv, norm_mem_q[i], m_wq, m_wo, i, norm_ffn[i])
        xf = _ffn(xf, h2, i, ffn_w_gate, ffn_w_up, ffn_wd)
    return _rmsnorm(xf, norm_final, F32).reshape(nbat, seq, d)
```

```python
import functools

import numpy as np
import jax
import jax.numpy as jnp
from jax import lax
from jax.experimental import pallas as pl
from jax.experimental.pallas import tpu as pltpu

F32 = jnp.float32
BF16 = jnp.bfloat16

V7X_LANES = 128
V7X_SUBLANES = 8
V7X_VMEM_BYTES = 64 * 2**20
V7X_VMEM_LIMIT_CAP = V7X_VMEM_BYTES - 6 * 2**20
COMPILER_SCRATCH_ALLOWANCE = 4 * 2**20

RMS_EPS = 1e-6
N_MIXERS = 3
CHUNK = 64
LRU_C = 8.0
LRU_CONV = 4
SSD_HEAD_DIM = 64
SSD_STATE = 128
SSD_GROUPS = 8
SSD_CONV = 4
SSD_CONV_LANES = 4 * V7X_LANES
ATTN_HEAD_DIM = 128
LOOKBACK_CHUNKS = 8
MAX_REL_DIST = 256
ATTN_QCHUNKS = 4
ATTN_WINDOW = (LOOKBACK_CHUNKS + 2) * CHUNK
MEM_HEADS = 4
MEM_HEAD_DIM = 128
CONV_PAD_ROWS = V7X_SUBLANES
NORM_ROWS = 64
ROW_BLOCK = V7X_SUBLANES * V7X_SUBLANES


def _tile(dim, target, align):
    if dim <= target:
        return dim
    best = None
    for t in range(align, target + 1, align):
        if dim % t == 0:
            best = t
    assert best is not None, (dim, target, align)
    return best


def _params(semantics, vmem_bytes):
    limit = vmem_bytes + COMPILER_SCRATCH_ALLOWANCE
    return pltpu.CompilerParams(
        dimension_semantics=semantics,
        vmem_limit_bytes=int(min(max(limit, 16 * 2**20), V7X_VMEM_LIMIT_CAP)),
    )


def _nbytes(shape, dtype):
    return int(np.prod(shape)) * jnp.dtype(dtype).itemsize


def _rms(x, g):
    ms = jnp.mean(x * x, axis=-1, keepdims=True)
    return x * lax.rsqrt(ms + RMS_EPS) * g


def _silu(x):
    hx = 0.5 * x
    return hx * jnp.tanh(hx) + hx


def _softmax_rows(s):
    m = jnp.max(s, axis=-1, keepdims=True)
    e = jnp.exp(s - m)
    return e / jnp.sum(e, axis=-1, keepdims=True)


def _rmsnorm_kernel(x_ref, g_ref, o_ref):
    o_ref[...] = _rms(x_ref[...], g_ref[...]).astype(o_ref.dtype)


def _rmsnorm(x, g, out_dtype):
    n, d = x.shape
    bm = _tile(n, 512, V7X_SUBLANES)
    vmem = 2 * (_nbytes((bm, d), F32) + _nbytes((bm, d), out_dtype)) + 2 * _nbytes((bm, d), F32)
    return pl.pallas_call(
        _rmsnorm_kernel,
        out_shape=jax.ShapeDtypeStruct((n, d), out_dtype),
        grid=(n // bm,),
        in_specs=[pl.BlockSpec((bm, d), lambda i: (i, 0)),
                  pl.BlockSpec((1, d), lambda i: (0, 0))],
        out_specs=pl.BlockSpec((bm, d), lambda i: (i, 0)),
        compiler_params=_params(("parallel",), vmem),
        name="rmsnorm",
    )(x, g.reshape(1, d))


def _mm_kernel(x_ref, w_ref, *rest, has_bias, has_res, act, scale):
    rest = list(rest)
    b_ref = rest.pop(0) if has_bias else None
    r_ref = rest.pop(0) if has_res else None
    (o_ref,) = rest
    acc = jnp.dot(x_ref[...], w_ref[...], preferred_element_type=F32)
    if has_bias:
        acc = acc + b_ref[...]
    if scale is not None:
        acc = acc * scale
    if act == "gelu":
        acc = jax.nn.gelu(acc)
    if has_res:
        acc = acc + r_ref[...]
    o_ref[...] = acc.astype(o_ref.dtype)


def _matmul(x, w, *, layer=0, col0=0, ncols=None, bias=None, residual=None, act=None, scale=None,
            out_dtype=F32, bm=1024, bn=1024, name="matmul"):
    m, kdim = x.shape
    n = w.shape[2] - col0 if ncols is None else ncols
    bm = _tile(m, bm, V7X_SUBLANES * 2)
    bn = _tile(n, bn, V7X_LANES)
    assert col0 % bn == 0 and w.shape[1] == kdim
    jb0 = col0 // bn
    in_specs = [pl.BlockSpec((bm, kdim), lambda i, j: (i, 0)),
                pl.BlockSpec((None, kdim, bn), lambda i, j: (layer, 0, jb0 + j))]
    args = [x, w]
    vmem = 2 * (_nbytes((bm, kdim), x.dtype) + _nbytes((kdim, bn), w.dtype) + _nbytes((bm, bn), out_dtype))
    vmem += (2 + 2 * (act is not None)) * _nbytes((bm, bn), F32)
    if bias is not None:
        in_specs.append(pl.BlockSpec((1, bn), lambda i, j: (0, j)))
        args.append(bias.reshape(1, n).astype(F32))
    if residual is not None:
        in_specs.append(pl.BlockSpec((bm, bn), lambda i, j: (i, j)))
        args.append(residual)
        vmem += 2 * _nbytes((bm, bn), residual.dtype)
    return pl.pallas_call(
        functools.partial(_mm_kernel, has_bias=bias is not None,
                          has_res=residual is not None, act=act, scale=scale),
        out_shape=jax.ShapeDtypeStruct((m, n), out_dtype),
        grid=(m // bm, n // bn),
        in_specs=in_specs,
        out_specs=pl.BlockSpec((bm, bn), lambda i, j: (i, j)),
        compiler_params=_params(("parallel", "parallel"), vmem),
        name=name,
    )(*args)


def _swiglu_up_kernel(x_ref, wg_ref, wu_ref, o_ref, wgb_ref, wub_ref):
    @pl.when(pl.program_id(1) == 0)
    def _():
        wgb_ref[...] = wg_ref[...].astype(BF16)
        wub_ref[...] = wu_ref[...].astype(BF16)

    x = x_ref[...]
    g = jnp.dot(x, wgb_ref[...], preferred_element_type=F32)
    u = jnp.dot(x, wub_ref[...], preferred_element_type=F32)
    o_ref[...] = (_silu(g) * u).astype(o_ref.dtype)


def _swiglu_up(x, wg, wu, layer, *, bm=1024, bn=256):
    m, kdim = x.shape
    n = wg.shape[2]
    bm = _tile(m, bm, V7X_SUBLANES * 2)
    bn = _tile(n, bn, V7X_LANES)
    vmem = 2 * (_nbytes((bm, kdim), BF16) + 2 * _nbytes((kdim, bn), F32) + _nbytes((bm, bn), BF16))
    vmem += 2 * _nbytes((kdim, bn), BF16) + 5 * _nbytes((bm, bn), F32) + 2 * _nbytes((kdim, bn), F32)
    wspec = pl.BlockSpec((None, kdim, bn), lambda j, i: (layer, 0, j))
    return pl.pallas_call(
        _swiglu_up_kernel,
        out_shape=jax.ShapeDtypeStruct((m, n), BF16),
        grid=(n // bn, m // bm),
        in_specs=[pl.BlockSpec((bm, kdim), lambda j, i: (i, 0)), wspec, wspec],
        out_specs=pl.BlockSpec((bm, bn), lambda j, i: (i, j)),
        scratch_shapes=[pltpu.VMEM((kdim, bn), BF16), pltpu.VMEM((kdim, bn), BF16)],
        compiler_params=_params(("parallel", "arbitrary"), vmem),
        name="swiglu_up",
    )(x, wg, wu)


def _memattn_kernel(x_ref, gq_ref, wq_ref, k_ref, v_ref, wo_ref, gf_ref, xo_ref, ho_ref):
    x = x_ref[...]
    h = _rms(x, gq_ref[...]).astype(BF16)
    q = jnp.dot(h, wq_ref[...], preferred_element_type=F32).astype(BF16)
    k = k_ref[0]
    v = v_ref[0]
    scale = MEM_HEAD_DIM ** -0.5
    outs = []
    for hh in range(MEM_HEADS):
        sl = slice(hh * MEM_HEAD_DIM, (hh + 1) * MEM_HEAD_DIM)
        s = lax.dot_general(q[:, sl], k[:, sl], (((1,), (1,)), ((), ())),
                            preferred_element_type=F32) * scale
        p = _softmax_rows(s).astype(BF16)
        outs.append(jnp.dot(p, v[:, sl], preferred_element_type=F32).astype(BF16))
    o = jnp.concatenate(outs, axis=1)
    x2 = x + jnp.dot(o, wo_ref[...], preferred_element_type=F32)
    xo_ref[...] = x2
    ho_ref[...] = _rms(x2, gf_ref[...]).astype(BF16)


def _mem_xattn(x, rows_per_batch, kv, g_q, wq, wo, layer, g_ffn):
    n, d = x.shape
    nb, tm, width2 = kv.shape
    width = width2 // 2
    bm = _tile(rows_per_batch, 256, V7X_SUBLANES * 2)
    per_batch = rows_per_batch // bm
    vmem = 2 * (2 * _nbytes((bm, d), F32) + _nbytes((bm, d), BF16) + 2 * _nbytes((d, width), BF16)
                + 2 * _nbytes((tm, width), BF16))
    vmem += 4 * _nbytes((bm, d), F32)
    return pl.pallas_call(
        _memattn_kernel,
        out_shape=(jax.ShapeDtypeStruct((n, d), F32), jax.ShapeDtypeStruct((n, d), BF16)),
        grid=(n // bm,),
        in_specs=[pl.BlockSpec((bm, d), lambda i: (i, 0)),
                  pl.BlockSpec((1, d), lambda i: (0, 0)),
                  pl.BlockSpec((None, d, width), lambda i: (layer, 0, 0)),
                  pl.BlockSpec((1, tm, width), lambda i: (i // per_batch, 0, 0)),
                  pl.BlockSpec((1, tm, width), lambda i: (i // per_batch, 0, 1)),
                  pl.BlockSpec((None, width, d), lambda i: (layer, 0, 0)),
                  pl.BlockSpec((1, d), lambda i: (0, 0))],
        out_specs=(pl.BlockSpec((bm, d), lambda i: (i, 0)),
                   pl.BlockSpec((bm, d), lambda i: (i, 0))),
        compiler_params=_params(("parallel",), vmem),
        name="mem_xattn",
    )(x, g_q.reshape(1, d), wq, kv, kv, wo, g_ffn.reshape(1, d))


def _strided_rows(ref, slabs, row0, r):
    return jnp.concatenate(
        [ref[s, pl.ds(row0 + r, V7X_SUBLANES, stride=V7X_SUBLANES), :] for s in slabs], axis=1)


def _store_strided_rows(ref, slabs, row0, r, val):
    for k, s in enumerate(slabs):
        ref[s, pl.ds(row0 + r, V7X_SUBLANES, stride=V7X_SUBLANES), :] = val[:, k * V7X_LANES:(k + 1) * V7X_LANES]


def _conv_row_block(xs_ref, slabs, base, cw, cb, ntaps):
    nsl = V7X_SUBLANES
    ext = {r: _strided_rows(xs_ref, slabs, base, r) for r in range(nsl)}
    row = lax.broadcasted_iota(jnp.int32, ext[0].shape, 0)
    for k in range(1, ntaps):
        above = jnp.concatenate([xs_ref[s, pl.ds(base - k, 1), :] for s in slabs], axis=1)
        ext[-k] = jnp.where(row >= 1, pltpu.roll(ext[nsl - k], 1, axis=0), above)
    out = []
    for r in range(nsl):
        y = cb
        for j in range(ntaps):
            y = y + cw[j:j + 1, :] * ext[r - (ntaps - 1) + j]
        out.append(y)
    return out


def _lru_kernel(xin_ref, gate_ref, cw_ref, cb_ref, wax_ref, bax_ref, lam_ref, o_ref,
                xs_ref, xc_ref, a_ref, u_ref, h_ref, *, tb, nblk, blk):
    t = pl.program_id(2)
    pad = CONV_PAD_ROWS
    lanes = V7X_LANES
    nsl = V7X_SUBLANES
    nslab = nblk * blk // lanes
    per_blk = blk // lanes
    nrb = tb // ROW_BLOCK

    @pl.when(t == 0)
    def _():
        xs_ref[:, 0:pad, :] = jnp.zeros((nslab, pad, lanes), F32)
        h_ref[...] = jnp.zeros(h_ref.shape, F32)

    @pl.when(t > 0)
    def _():
        xs_ref[:, 0:pad, :] = xs_ref[:, tb:tb + pad, :]

    for s in range(nslab):
        xs_ref[s, pad:pad + tb, :] = xin_ref[:, s * lanes:(s + 1) * lanes]

    for nb in range(nblk):
        slabs = list(range(nb * per_blk, (nb + 1) * per_blk))
        sl = slice(nb * blk, (nb + 1) * blk)
        cw = cw_ref[:, sl]
        cb = cb_ref[:, sl]

        def conv_body(b, carry, slabs=slabs, cw=cw, cb=cb):
            r0 = pl.multiple_of(b * ROW_BLOCK, ROW_BLOCK)
            ys = _conv_row_block(xs_ref, slabs, pad + r0, cw, cb, LRU_CONV)
            for r in range(nsl):
                _store_strided_rows(xc_ref, slabs, r0, r, ys[r])
            return carry

        lax.fori_loop(0, nrb, conv_body, 0, unroll=2)

    half_neg_sp = (-0.5 * LRU_C) * jax.nn.softplus(-lam_ref[...])
    for nb in range(nblk):
        slabs = list(range(nb * per_blk, (nb + 1) * per_blk))
        sl = slice(nb * blk, (nb + 1) * blk)
        xcb = jnp.concatenate([xc_ref[s] for s in slabs], axis=1)
        half_gates = jnp.dot(xcb.astype(BF16), wax_ref[nb], preferred_element_type=F32) + bax_ref[nb]
        tr = jnp.tanh(half_gates[:, :blk])
        ig = 0.5 * jnp.tanh(half_gates[:, blk:]) + 0.5
        log_a = half_neg_sp[:, sl] * tr + half_neg_sp[:, sl]
        th = jnp.tanh(log_a)
        n = -2.0 * th
        root = jnp.where(n > 0.0, n * lax.rsqrt(n * (1.0 - th)), 0.0)
        a = jnp.exp(log_a)
        u = root * (ig * xcb)
        for k, s in enumerate(slabs):
            a_ref[s] = a[:, k * lanes:(k + 1) * lanes]
            u_ref[s] = u[:, k * lanes:(k + 1) * lanes]

    for nb in range(nblk):
        slabs = list(range(nb * per_blk, (nb + 1) * per_blk))
        sl = slice(nb * blk, (nb + 1) * blk)
        row = lax.broadcasted_iota(jnp.int32, (nsl, blk), 0)

        def scan_body(b, h, slabs=slabs, row=row):
            r0 = pl.multiple_of(b * ROW_BLOCK, ROW_BLOCK)
            acc_a, acc_h = [], []
            for r in range(nsl):
                a = _strided_rows(a_ref, slabs, r0, r)
                u = _strided_rows(u_ref, slabs, r0, r)
                if r > 0:
                    u = a * acc_h[-1] + u
                    a = a * acc_a[-1]
                acc_a.append(a)
                acc_h.append(u)
            ga, gh = acc_a[-1], acc_h[-1]
            for sh in (1, 2, 4):
                ga_prev = jnp.where(row >= sh, pltpu.roll(ga, sh, axis=0), 1.0)
                gh_prev = jnp.where(row >= sh, pltpu.roll(gh, sh, axis=0), 0.0)
                gh = ga * gh_prev + gh
                ga = ga * ga_prev
            ends = gh + ga * h
            enter = jnp.where(row >= 1, pltpu.roll(ends, 1, axis=0), h)
            for r in range(nsl):
                _store_strided_rows(u_ref, slabs, r0, r, acc_h[r] + acc_a[r] * enter)
            return ends[nsl - 1:nsl, :]

        h_ref[:, sl] = lax.fori_loop(0, nrb, scan_body, h_ref[:, sl], unroll=2)

    hs = jnp.concatenate([u_ref[s] for s in range(nslab)], axis=1)
    o_ref[...] = (hs * gate_ref[...].astype(F32)).astype(o_ref.dtype)


def _lru_core(xin, gate, rows_per_batch, conv_w, conv_b, w_ax, b_ax, lam):
    n, r = xin.shape
    nblocks, blk = w_ax.shape[0], w_ax.shape[1]
    nblk = _tile(nblocks, 3, 1)
    c = nblk * blk
    tb = _tile(rows_per_batch, 512, ROW_BLOCK)
    nt = rows_per_batch // tb
    nbat = n // rows_per_batch
    pad = CONV_PAD_ROWS
    nslab = c // V7X_LANES
    assert blk % V7X_LANES == 0 and tb % ROW_BLOCK == 0
    vmem = 2 * (_nbytes((tb, c), F32) + 2 * _nbytes((tb, c), BF16) + _nbytes((nblk, blk, 2 * blk), BF16))
    vmem += _nbytes((tb + pad, c), F32) + 3 * _nbytes((tb, c), F32) + 6 * _nbytes((tb, c), F32)
    return pl.pallas_call(
        functools.partial(_lru_kernel, tb=tb, nblk=nblk, blk=blk),
        out_shape=jax.ShapeDtypeStruct((n, r), BF16),
        grid=(nbat, r // c, nt),
        in_specs=[pl.BlockSpec((tb, c), lambda b, j, t: (b * nt + t, j)),
                  pl.BlockSpec((tb, c), lambda b, j, t: (b * nt + t, j)),
                  pl.BlockSpec((LRU_CONV, c), lambda b, j, t: (0, j)),
                  pl.BlockSpec((1, c), lambda b, j, t: (0, j)),
                  pl.BlockSpec((nblk, blk, 2 * blk), lambda b, j, t: (j, 0, 0)),
                  pl.BlockSpec((nblk, 1, 2 * blk), lambda b, j, t: (j, 0, 0)),
                  pl.BlockSpec((1, c), lambda b, j, t: (0, j))],
        out_specs=pl.BlockSpec((tb, c), lambda b, j, t: (b * nt + t, j)),
        scratch_shapes=[pltpu.VMEM((nslab, tb + pad, V7X_LANES), F32),
                        pltpu.VMEM((nslab, tb, V7X_LANES), F32),
                        pltpu.VMEM((nslab, tb, V7X_LANES), F32),
                        pltpu.VMEM((nslab, tb, V7X_LANES), F32),
                        pltpu.VMEM((1, c), F32)],
        compiler_params=_params(("parallel", "parallel", "arbitrary"), vmem),
        name="lru_core",
    )(xin, gate, conv_w, conv_b.reshape(1, r), w_ax, b_ax, lam.reshape(1, r))


def _lru_layer(x, h, rows_per_batch, layer, w_in, b_in, w_gate, b_gate, conv_w, conv_b,
               w_a, b_a, w_x, b_x, lam, w_out, b_out):
    nblocks, blk = w_a.shape[0], w_a.shape[1]
    gate = _matmul(h, w_gate, layer=layer, bias=b_gate, act="gelu", out_dtype=BF16,
                   bn=768, name="lru_gate")
    xin = _matmul(h, w_in, layer=layer, bias=b_in, out_dtype=F32, bn=768, name="lru_in")
    w_ax = (0.5 * jnp.concatenate([w_a, w_x], axis=-1)).astype(BF16)
    b_ax = 0.5 * jnp.concatenate([b_a.reshape(nblocks, 1, blk), b_x.reshape(nblocks, 1, blk)], axis=-1)
    hs = _lru_core(xin, gate, rows_per_batch, conv_w, conv_b, w_ax, b_ax, lam)
    return _matmul(hs, w_out, layer=layer, bias=b_out, residual=x, out_dtype=F32,
                   bm=512, name="lru_out")


def _split3(v):
    hi = v.astype(BF16)
    r1 = v - hi.astype(F32)
    mid = r1.astype(BF16)
    lo = (r1 - mid.astype(F32)).astype(BF16)
    return hi, mid, lo


def _ssd_kernel(z_ref, xbc_ref, dtr_ref, cw_ref, cb_ref, dtb_ref, alog_ref, dsk_ref, nw_ref, y_ref,
                xs_ref, xa_ref, st_ref, ct2_ref, dt2_ref, wt2_ref, *, d_inner):
    c = pl.program_id(1)
    L = CHUNK
    pad = CONV_PAD_ROWS
    lanes = V7X_LANES
    half = lanes // 2
    gw = d_inner // SSD_GROUPS
    pairs_per_group = gw // lanes
    npair = d_inner // lanes
    assert SSD_HEAD_DIM == half and SSD_STATE == lanes and 2 * npair == lanes and L == half

    nslab = xs_ref.shape[0]
    slabs_per_pass = SSD_CONV_LANES // lanes
    assert L == ROW_BLOCK

    @pl.when(c == 0)
    def _():
        xs_ref[:, 0:pad, :] = jnp.zeros((nslab, pad, lanes), F32)
        st_ref[...] = jnp.zeros(st_ref.shape, F32)

    @pl.when(c > 0)
    def _():
        xs_ref[:, 0:pad, :] = xs_ref[:, L:L + pad, :]

    def conv_body(i, carry):
        l0 = pl.multiple_of(i * SSD_CONV_LANES, SSD_CONV_LANES)
        slabs = [i * slabs_per_pass + k for k in range(slabs_per_pass)]
        for k, s in enumerate(slabs):
            xs_ref[s, pad:pad + L, :] = xbc_ref[:, pl.ds(l0 + k * lanes, lanes)]
        ys = _conv_row_block(xs_ref, slabs, pad, cw_ref[:, pl.ds(l0, SSD_CONV_LANES)],
                             cb_ref[:, pl.ds(l0, SSD_CONV_LANES)], SSD_CONV)
        for r in range(V7X_SUBLANES):
            _store_strided_rows(xa_ref, slabs, 0, r, _silu(ys[r]))
        return carry

    lax.fori_loop(0, nslab // slabs_per_pass, conv_body, 0, unroll=2)

    dt = jax.nn.softplus(dtr_ref[...] + dtb_ref[...])
    da = dt * (-jnp.exp(alog_ref[...]))
    ri = lax.broadcasted_iota(jnp.int32, (L, L), 0)
    ci = lax.broadcasted_iota(jnp.int32, (L, L), 1)
    tril = jnp.where(ri >= ci, 1.0, 0.0).astype(BF16)
    parts = jnp.dot(tril, jnp.concatenate(_split3(da), axis=1), preferred_element_type=F32)
    cum = (parts[:, :lanes] + parts[:, lanes:2 * lanes]) + parts[:, 2 * lanes:]
    last = cum[L - 1:L, :]
    w_end = dt * jnp.exp(last - cum)
    dec_chunk = jnp.exp(last)

    def pair_rows(v):
        vt = v.T
        return jnp.concatenate([vt[:npair], vt[npair:]], axis=1)

    ct2_ref[...] = pair_rows(cum)
    dt2_ref[...] = pair_rows(dt)
    wt2_ref[...] = pair_rows(w_end)

    lane_i = lax.broadcasted_iota(jnp.int32, (L, lanes), 1)
    row_i = lax.broadcasted_iota(jnp.int32, (L, lanes), 0)
    lo_half = lane_i < half
    causal2 = row_i >= jnp.where(lo_half, lane_i, lane_i - half)
    lo_half_row = lax.broadcasted_iota(jnp.int32, (1, lanes), 1) < half
    lo_half_sq = lax.broadcasted_iota(jnp.int32, (lanes, lanes), 1) < half
    top_half_sq = lax.broadcasted_iota(jnp.int32, (lanes, lanes), 0) < half
    blockdiag = lo_half_sq == top_half_sq

    def group_body(g, carry):
        x0 = pl.multiple_of(g * gw, gw)
        p0 = pl.multiple_of(g * pairs_per_group, pairs_per_group)
        bg = xa_ref[npair + g]
        cg = xa_ref[npair + SSD_GROUPS + g].astype(BF16)
        b2 = jnp.concatenate([bg, bg], axis=0)
        cb2 = lax.dot_general(cg, b2.astype(BF16), (((1,), (1,)), ((), ())),
                              preferred_element_type=F32)
        bt2 = b2.T
        sg = st_ref[g]
        yoff = jnp.dot(cg, sg.astype(BF16), preferred_element_type=F32)
        shift = lax.rem(lanes - g * pairs_per_group, lanes)
        cum_r = pltpu.roll(cum, shift, axis=1)
        dec_r = pltpu.roll(dec_chunk, shift, axis=1)
        ct8 = ct2_ref[pl.ds(p0, pairs_per_group), :]
        dt8 = dt2_ref[pl.ds(p0, pairs_per_group), :]
        wt8 = wt2_ref[pl.ds(p0, pairs_per_group), :]
        xps = [xa_ref[p0 + pj] for pj in range(pairs_per_group)]
        xg = jnp.concatenate(xps, axis=1)
        ys, snew = [], []
        for pj in range(pairs_per_group):
            ls = slice(pj * lanes, (pj + 1) * lanes)
            xp = xps[pj]
            colp = jnp.where(lo_half,
                             jnp.broadcast_to(cum_r[:, pj:pj + 1], (L, lanes)),
                             jnp.broadcast_to(cum_r[:, half + pj:half + pj + 1], (L, lanes)))
            seg = jnp.where(causal2, colp - ct8[pj:pj + 1, :], -jnp.inf)
            m_diag = (cb2 * jnp.exp(seg) * dt8[pj:pj + 1, :]).astype(BF16)
            m_state = (bt2 * wt8[pj:pj + 1, :]).astype(BF16)
            x2 = jnp.concatenate([xp, xp], axis=0)
            xbd = jnp.where(blockdiag, x2, 0.0).astype(BF16)
            res = jnp.dot(jnp.concatenate([m_diag, m_state], axis=0), xbd,
                          preferred_element_type=F32)
            ys.append(res[:L] + jnp.exp(colp) * yoff[:, ls])
            decp = jnp.where(lo_half_row,
                             jnp.broadcast_to(dec_r[:, pj:pj + 1], (1, lanes)),
                             jnp.broadcast_to(dec_r[:, half + pj:half + pj + 1], (1, lanes)))
            snew.append(decp * sg[:, ls] + res[L:])
        st_ref[g] = jnp.concatenate(snew, axis=1)
        y = jnp.concatenate(ys, axis=1) + dsk_ref[:, pl.ds(x0, gw)] * xg
        y = y * _silu(z_ref[:, pl.ds(x0, gw)].astype(F32))
        y = _rms(y, nw_ref[:, pl.ds(x0, gw)])
        y_ref[:, pl.ds(x0, gw)] = y.astype(y_ref.dtype)
        return carry

    lax.fori_loop(0, SSD_GROUPS, group_body, 0, unroll=2)


def _ssd_core(z, xbc, dtr, rows_per_batch, conv_w, conv_b, dt_bias, a_log, d_rep, norm_w):
    n, d_inner = z.shape
    cdim = xbc.shape[1]
    nheads = dtr.shape[1]
    nc = rows_per_batch // CHUNK
    nbat = n // rows_per_batch
    gw = d_inner // SSD_GROUPS
    npair = d_inner // V7X_LANES
    pad = CONV_PAD_ROWS
    assert cdim % SSD_CONV_LANES == 0
    vmem = 2 * (2 * _nbytes((CHUNK, d_inner), BF16) + _nbytes((CHUNK, cdim), F32)
                + _nbytes((SSD_CONV + 1, cdim), F32))
    vmem += _nbytes((CHUNK + pad, cdim), F32) + _nbytes((CHUNK, cdim), F32)
    vmem += _nbytes((SSD_GROUPS, SSD_STATE, gw), F32) + 8 * _nbytes((SSD_STATE, gw), F32)
    row = lambda b, c: (b * nc + c, 0)
    const = lambda b, c: (0, 0)
    return pl.pallas_call(
        functools.partial(_ssd_kernel, d_inner=d_inner),
        out_shape=jax.ShapeDtypeStruct((n, d_inner), BF16),
        grid=(nbat, nc),
        in_specs=[pl.BlockSpec((CHUNK, d_inner), row),
                  pl.BlockSpec((CHUNK, cdim), row),
                  pl.BlockSpec((CHUNK, nheads), row),
                  pl.BlockSpec((SSD_CONV, cdim), const),
                  pl.BlockSpec((1, cdim), const),
                  pl.BlockSpec((1, nheads), const),
                  pl.BlockSpec((1, nheads), const),
                  pl.BlockSpec((1, d_inner), const),
                  pl.BlockSpec((1, d_inner), const)],
        out_specs=pl.BlockSpec((CHUNK, d_inner), row),
        scratch_shapes=[pltpu.VMEM((cdim // V7X_LANES, CHUNK + pad, V7X_LANES), F32),
                        pltpu.VMEM((cdim // V7X_LANES, CHUNK, V7X_LANES), F32),
                        pltpu.VMEM((SSD_GROUPS, SSD_STATE, gw), F32),
                        pltpu.VMEM((npair, V7X_LANES), F32),
                        pltpu.VMEM((npair, V7X_LANES), F32),
                        pltpu.VMEM((npair, V7X_LANES), F32)],
        compiler_params=_params(("parallel", "arbitrary"), vmem),
        name="ssd_core",
    )(z, xbc, dtr, conv_w, conv_b.reshape(1, cdim), dt_bias.reshape(1, nheads),
      a_log.reshape(1, nheads), d_rep.reshape(1, d_inner), norm_w.reshape(1, d_inner))


def _ssd_layer(x, h, rows_per_batch, layer, w_in, w_in_f32, conv_w, conv_b, dt_bias, a_log, d_skip,
               norm_w, w_out):
    nheads = dt_bias.shape[0]
    d_inner = nheads * SSD_HEAD_DIM
    cdim = conv_w.shape[1]
    perm = np.concatenate([np.arange(0, nheads, 2), np.arange(1, nheads, 2)])
    w_dt = w_in_f32[layer, :, d_inner + cdim:][:, perm].astype(BF16)[None]
    z = _matmul(h, w_in, layer=layer, ncols=d_inner, out_dtype=BF16, name="ssd_in_z")
    xbc = _matmul(h, w_in, layer=layer, col0=d_inner, ncols=cdim, out_dtype=F32, name="ssd_in_xbc")
    dtr = _matmul(h, w_dt, out_dtype=F32, name="ssd_in_dt")
    d_rep = jnp.repeat(d_skip.astype(F32), SSD_HEAD_DIM)
    y = _ssd_core(z, xbc, dtr, rows_per_batch, conv_w, conv_b, dt_bias[perm], a_log[perm], d_rep, norm_w)
    return _matmul(y, w_out, layer=layer, residual=x, out_dtype=F32, bm=512, bn=512, name="ssd_out")


def _bias_gather_kernel(idx_ref, tab_ref, o_ref, *, heads):
    idx = idx_ref[...]
    ntab = tab_ref.shape[1]
    rows = lax.broadcasted_iota(jnp.int32, (ntab, idx.shape[1]), 0)
    onehot = jnp.where(rows == idx, 1.0, 0.0).astype(BF16)
    r = jnp.dot(tab_ref[...], onehot, preferred_element_type=F32)
    o_ref[...] = (r[0:heads] + r[heads:2 * heads]) + r[2 * heads:3 * heads]


def _rel_bias(rel_table):
    heads, ntab = rel_table.shape
    pad = LOOKBACK_CHUNKS * CHUNK
    band = pad + CHUNK
    rel = (np.arange(CHUNK)[:, None] + pad) - np.arange(band)[None, :]
    idx = (np.clip(rel, -MAX_REL_DIST, MAX_REL_DIST) + MAX_REL_DIST).astype(np.int32).reshape(1, -1)
    npos = idx.shape[1]
    ntab_p = -(-ntab // V7X_LANES) * V7X_LANES
    tab = jnp.pad(rel_table.astype(F32), ((0, 0), (0, ntab_p - ntab)))
    tab3 = jnp.concatenate(_split3(tab), axis=0)
    tp = _tile(npos, 2048, V7X_LANES)
    out = pl.pallas_call(
        functools.partial(_bias_gather_kernel, heads=heads),
        out_shape=jax.ShapeDtypeStruct((heads, npos), F32),
        grid=(npos // tp,),
        in_specs=[pl.BlockSpec((1, tp), lambda i: (0, i)),
                  pl.BlockSpec((3 * heads, ntab_p), lambda i: (0, 0))],
        out_specs=pl.BlockSpec((heads, tp), lambda i: (0, i)),
        compiler_params=_params(("parallel",), 16 * 2**20),
        name="rel_bias_gather",
    )(jnp.asarray(idx), tab3)
    bias = out.reshape(heads, CHUNK, band)
    masked = jnp.full((heads, CHUNK, CHUNK), -jnp.inf, F32)
    return jnp.stack([jnp.concatenate([bias, masked], axis=2),
                      jnp.concatenate([masked, bias], axis=2)], axis=1)


def _attn_kernel(q_ref, kp_ref, kc_ref, vp_ref, vc_ref, bias_ref, o_ref, kb_ref, vb_ref, s_ref, p_ref,
                 *, tq, hg):
    t = pl.program_id(1)
    pad = LOOKBACK_CHUNKS * CHUNK
    hd = ATTN_HEAD_DIM
    qb = ATTN_QCHUNKS * CHUNK
    kbw = pad + qb
    assert tq == pad and ATTN_QCHUNKS % 2 == 0 and ATTN_WINDOW == pad + 2 * CHUNK
    kb_ref[0:pad, :] = kp_ref[...]
    kb_ref[pad:pad + tq, :] = kc_ref[...]
    vb_ref[0:pad, :] = vp_ref[...]
    vb_ref[pad:pad + tq, :] = vc_ref[...]
    col = lax.broadcasted_iota(jnp.int32, (CHUNK, ATTN_WINDOW), 1)

    def block_body(sb, carry):
        r0 = pl.multiple_of(sb * qb, qb)
        first_key = t * tq + sb * qb - pad
        for h in range(hg):
            ls = slice(h * hd, (h + 1) * hd)
            qh = q_ref[pl.ds(r0, qb), ls]
            kb = kb_ref[pl.ds(r0, kbw), ls]
            s_ref[h] = lax.dot_general(qh, kb, (((1,), (1,)), ((), ())), preferred_element_type=F32)
        for h in range(hg):
            for c in range(ATTN_QCHUNKS):
                w0 = (c // 2) * 2 * CHUNK
                rows = slice(c * CHUNK, (c + 1) * CHUNK)
                sc = s_ref[h, rows, w0:w0 + ATTN_WINDOW] + bias_ref[h, c % 2]
                sc = jnp.where(first_key + w0 + col >= 0, sc, -jnp.inf)
                p_ref[h, rows, w0:w0 + ATTN_WINDOW] = _softmax_rows(sc).astype(BF16)
                if w0 > 0:
                    p_ref[h, rows, 0:w0] = jnp.zeros((CHUNK, w0), BF16)
                if w0 + ATTN_WINDOW < kbw:
                    p_ref[h, rows, w0 + ATTN_WINDOW:kbw] = jnp.zeros((CHUNK, kbw - w0 - ATTN_WINDOW), BF16)
        for h in range(hg):
            ls = slice(h * hd, (h + 1) * hd)
            vb = vb_ref[pl.ds(r0, kbw), ls]
            o = jnp.dot(p_ref[h], vb, preferred_element_type=F32)
            o_ref[pl.ds(r0, qb), ls] = o.astype(o_ref.dtype)
        return carry

    lax.fori_loop(0, tq // qb, block_body, 0, unroll=True)


def _attn_core(q, k, v, bias, rows_per_batch):
    n, d = q.shape
    heads = d // ATTN_HEAD_DIM
    tq = LOOKBACK_CHUNKS * CHUNK
    nt = rows_per_batch // tq
    nbat = n // rows_per_batch
    hg = _tile(heads, 8, 1)
    gwid = hg * ATTN_HEAD_DIM
    kbw = tq + ATTN_QCHUNKS * CHUNK
    cur = lambda b, t, g: (b * nt + t, g)
    prev = lambda b, t, g: (b * nt + jnp.maximum(t - 1, 0), g)
    vmem = 2 * (6 * _nbytes((tq, gwid), BF16) + _nbytes((hg, 2, CHUNK, ATTN_WINDOW), F32))
    vmem += 2 * _nbytes((2 * tq, gwid), BF16) + 4 * hg * _nbytes((ATTN_QCHUNKS * CHUNK, kbw), F32)
    return pl.pallas_call(
        functools.partial(_attn_kernel, tq=tq, hg=hg),
        out_shape=jax.ShapeDtypeStruct((n, d), BF16),
        grid=(nbat, nt, heads // hg),
        in_specs=[pl.BlockSpec((tq, gwid), cur),
                  pl.BlockSpec((tq, gwid), prev),
                  pl.BlockSpec((tq, gwid), cur),
                  pl.BlockSpec((tq, gwid), prev),
                  pl.BlockSpec((tq, gwid), cur),
                  pl.BlockSpec((hg, 2, CHUNK, ATTN_WINDOW), lambda b, t, g: (g, 0, 0, 0))],
        out_specs=pl.BlockSpec((tq, gwid), cur),
        scratch_shapes=[pltpu.VMEM((2 * tq, gwid), BF16),
                        pltpu.VMEM((2 * tq, gwid), BF16),
                        pltpu.VMEM((hg, ATTN_QCHUNKS * CHUNK, kbw), F32),
                        pltpu.VMEM((hg, ATTN_QCHUNKS * CHUNK, kbw), BF16)],
        compiler_params=_params(("parallel", "parallel", "parallel"), vmem),
        name="band_attn",
    )(q, k, k, v, v, bias)


def _norm_qkv_kernel(x_ref, g_ref, wq_ref, wk_ref, wv_ref, q_ref, k_ref, v_ref, h_ref, *, scale):
    @pl.when(pl.program_id(1) == 0)
    def _():
        def norm_rows(i, carry):
            r0 = pl.multiple_of(i * NORM_ROWS, NORM_ROWS)
            h_ref[pl.ds(r0, NORM_ROWS), :] = _rms(x_ref[pl.ds(r0, NORM_ROWS), :], g_ref[...]).astype(BF16)
            return carry

        lax.fori_loop(0, x_ref.shape[0] // NORM_ROWS, norm_rows, 0)

    h = h_ref[...]
    q_ref[...] = (jnp.dot(h, wq_ref[...], preferred_element_type=F32) * scale).astype(q_ref.dtype)
    k_ref[...] = jnp.dot(h, wk_ref[...], preferred_element_type=F32).astype(k_ref.dtype)
    v_ref[...] = jnp.dot(h, wv_ref[...], preferred_element_type=F32).astype(v_ref.dtype)


def _norm_qkv(x, g, wq, wk, wv, layer, *, bm=512, bn=512):
    m, d = x.shape
    n = wq.shape[2]
    bm = _tile(m, bm, V7X_SUBLANES * 2)
    bn = _tile(n, bn, V7X_LANES)
    vmem = 2 * (_nbytes((bm, d), F32) + 3 * _nbytes((d, bn), BF16) + 3 * _nbytes((bm, bn), BF16))
    vmem += _nbytes((bm, d), BF16) + 2 * _nbytes((bm, d), F32) + 3 * _nbytes((bm, bn), F32)
    wspec = pl.BlockSpec((None, d, bn), lambda i, j: (layer, 0, j))
    ospec = pl.BlockSpec((bm, bn), lambda i, j: (i, j))
    return pl.pallas_call(
        functools.partial(_norm_qkv_kernel, scale=ATTN_HEAD_DIM ** -0.5),
        out_shape=[jax.ShapeDtypeStruct((m, n), BF16)] * 3,
        grid=(m // bm, n // bn),
        in_specs=[pl.BlockSpec((bm, d), lambda i, j: (i, 0)),
                  pl.BlockSpec((1, d), lambda i, j: (0, 0)), wspec, wspec, wspec],
        out_specs=[ospec, ospec, ospec],
        scratch_shapes=[pltpu.VMEM((bm, d), BF16)],
        compiler_params=_params(("parallel", "arbitrary"), vmem),
        name="attn_norm_qkv",
    )(x, g.reshape(1, d), wq, wk, wv)


def _attn_layer(x, g_mix, rows_per_batch, layer, wq, wk, wv, wo, rel_table):
    q, k, v = _norm_qkv(x, g_mix, wq, wk, wv, layer)
    o = _attn_core(q, k, v, _rel_bias(rel_table), rows_per_batch)
    return _matmul(o, wo, layer=layer, residual=x, out_dtype=F32, name="attn_out")


def _ffn(x, h, layer, w_gate, w_up, w_down):
    act = _swiglu_up(h, w_gate, w_up, layer)
    return _matmul(act, w_down, layer=layer, residual=x, out_dtype=F32, bm=512, bn=512, name="ffn_down")


def kernel(x, mem, norm_mix, norm_mem_q, norm_mem_kv, norm_ffn, norm_final, mem_wq, mem_wk, mem_wv, mem_wo, ffn_w_gate, ffn_w_up, ffn_w_down, lru_w_in, lru_b_in, lru_w_gate, lru_b_gate, lru_conv_w, lru_conv_b, lru_w_a, lru_b_a, lru_w_x, lru_b_x, lru_lambda, lru_w_out, lru_b_out, ssd_w_in, ssd_conv_w, ssd_conv_b, ssd_dt_bias, ssd_a_log, ssd_d, ssd_norm, ssd_w_out, attn_wq, attn_wk, attn_wv, attn_wo, attn_rel_bias):
    nbat, seq, d = x.shape
    tm = mem.shape[1]
    depth = norm_mix.shape[0]
    xf = x.reshape(nbat * seq, d)
    memf = mem.reshape(nbat * tm, d)
    bf = lambda w: w.astype(BF16)
    lru_wi, lru_wg, lru_wo = bf(lru_w_in), bf(lru_w_gate), bf(lru_w_out)
    ssd_wi, ssd_wo = bf(ssd_w_in), bf(ssd_w_out)
    a_wq, a_wk, a_wv, a_wo = bf(attn_wq), bf(attn_wk), bf(attn_wv), bf(attn_wo)
    m_wq, m_wo = bf(mem_wq), bf(mem_wo)
    m_wkv = bf(jnp.concatenate([mem_wk, mem_wv], axis=2))
    ffn_wd = bf(ffn_w_down)
    for i in range(depth):
        kind, j = i % N_MIXERS, i // N_MIXERS
        if kind != 2:
            h = _rmsnorm(xf, norm_mix[i], BF16)
        if kind == 0:
            xf = _lru_layer(xf, h, seq, j, lru_wi, lru_b_in[j], lru_wg, lru_b_gate[j],
                            lru_conv_w[j], lru_conv_b[j], lru_w_a[j], lru_b_a[j], lru_w_x[j], lru_b_x[j],
                            lru_lambda[j], lru_wo, lru_b_out[j])
        elif kind == 1:
            xf = _ssd_layer(xf, h, seq, j, ssd_wi, ssd_w_in, ssd_conv_w[j], ssd_conv_b[j], ssd_dt_bias[j],
                            ssd_a_log[j], ssd_d[j], ssd_norm[j], ssd_wo)
        else:
            xf = _attn_layer(xf, norm_mix[i], seq, j, a_wq, a_wk, a_wv, a_wo, attn_rel_bias[j])
        hm = _rmsnorm(memf, norm_mem_kv[i], BF16)
        kv = _matmul(hm, m_wkv, layer=i, out_dtype=BF16, name="mem_kv").reshape(nbat, tm, -1)
        xf, h2 = _mem_xattn(xf, seq, kv, norm_mem_q[i], m_wq, m_wo, i, norm_ffn[i])
        xf = _ffn(xf, h2, i, ffn_w_gate, ffn_w_up, ffn_wd)
    return _rmsnorm(xf, norm_final, F32).reshape(nbat, seq, d)
```
